```python
import jax, jax.numpy as jnp
from jax import lax
import numpy as np

D_MODEL = 1024
BATCH = 2
SEQ = 16384
DEPTH = 2

HG_HEADS = 8
HG_KEY_DIM = 128
HG_VAL_DIM = D_MODEL // HG_HEADS
HG_KEY = HG_HEADS * HG_KEY_DIM
HG_VAL = HG_HEADS * HG_VAL_DIM
HG_CHUNK = 64
SG_GROUPS = 8
SG_GROUP_DIM = 64
SG_WIDTH = SG_GROUPS * SG_GROUP_DIM
SG_CHUNK = 128
FFN_HIDDEN = ((8 * D_MODEL // 3 + 255) // 256) * 256
PLE_DIM = 256
EPS = 1e-6
IN_SPLITS = (HG_KEY, HG_KEY, HG_KEY, HG_VAL, HG_VAL, SG_WIDTH, SG_WIDTH, D_MODEL, D_MODEL)
N_IN = HG_KEY * 3 + HG_VAL * 2 + SG_WIDTH * 2 + D_MODEL * 2

kernel_name = "hgrn2_gmlp_gated_hybrid_encoder"


def rms_norm(x, g):
    xf = x.astype(jnp.float32)
    y = xf * lax.rsqrt(jnp.mean(xf * xf, axis=-1, keepdims=True) + EPS)
    return (y * g.astype(jnp.float32)).astype(x.dtype)


def layer_norm(x, g, b):
    xf = x.astype(jnp.float32)
    mu = jnp.mean(xf, axis=-1, keepdims=True)
    xc = xf - mu
    y = xc * lax.rsqrt(jnp.mean(xc * xc, axis=-1, keepdims=True) + EPS)
    return (y * g.astype(jnp.float32) + b.astype(jnp.float32)).astype(x.dtype)


def layer_lower_bounds(gamma):
    sm = jax.nn.softmax(gamma.astype(jnp.float32), axis=0)
    return jnp.cumsum(sm, axis=0) - sm[0:1]


def _to_chunks(t):
    b, s, h, d = t.shape
    return t.reshape(b, s // HG_CHUNK, HG_CHUNK, h, d).transpose(1, 0, 3, 2, 4)


def hgrn2_direction(q, k, v, logf):
    bsz, s, h, dk = q.shape
    dv = v.shape[-1]
    mask = jnp.tril(jnp.ones((HG_CHUNK, HG_CHUNK), dtype=jnp.float32))

    def step(state, inp):
        qc, kc, vc, gc = inp
        b = jnp.cumsum(gc, axis=2)
        o_inter = jnp.einsum('bhtk,bhkv->bhtv', qc * jnp.exp(b), state)
        diff = b[:, :, :, None, :] - b[:, :, None, :, :]
        decay = jnp.exp(jnp.minimum(diff, 0.0)) * mask[:, :, None]
        scores = jnp.einsum('bhtk,bhsk,bhtsk->bhts', qc, kc, decay)
        o_intra = jnp.einsum('bhts,bhsv->bhtv', scores, vc)
        b_last = b[:, :, -1:, :]
        new_state = (jnp.exp(b_last[:, :, 0, :])[..., None] * state
                     + jnp.einsum('bhsk,bhsv->bhkv', kc * jnp.exp(b_last - b), vc))
        return new_state, o_inter + o_intra

    init = jnp.zeros((bsz, h, dk, dv), jnp.float32)
    _, o = lax.scan(step, init, (_to_chunks(q), _to_chunks(k), _to_chunks(v), _to_chunks(logf)))
    return o.transpose(1, 0, 3, 2, 4).reshape(bsz, s, h, dv)


def hgrn2_mixer(zq, zf_fwd, zf_bwd, zi, zg, lb_f, lb_b, norm_g):
    bsz, s, _ = zq.shape
    f32 = jnp.float32
    tiny = jnp.finfo(f32).tiny
    q = jax.nn.silu(zq.astype(f32)).reshape(bsz, s, HG_HEADS, HG_KEY_DIM)
    v = zi.astype(f32).reshape(bsz, s, HG_HEADS, HG_VAL_DIM)

    def gates(zf, lb):
        zf = zf.astype(f32)
        f = lb + (1.0 - lb) * jax.nn.sigmoid(zf)
        logf = jnp.log(jnp.maximum(f, tiny))
        k = (1.0 - lb) * jax.nn.sigmoid(-zf)
        return (k.reshape(bsz, s, HG_HEADS, HG_KEY_DIM), logf.reshape(bsz, s, HG_HEADS, HG_KEY_DIM))

    k_f, logf_f = gates(zf_fwd, lb_f)
    k_b, logf_b = gates(zf_bwd, lb_b)
    o_fwd = hgrn2_direction(q, k_f, v, logf_f)
    o_bwd = hgrn2_direction(q[:, ::-1], k_b[:, ::-1], v[:, ::-1], logf_b[:, ::-1])[:, ::-1]
    o = o_fwd + o_bwd
    o = rms_norm(o, norm_g.reshape(HG_HEADS, HG_VAL_DIM)).reshape(bsz, s, HG_VAL)
    return (o * jax.nn.silu(zg.astype(f32))).astype(zq.dtype)


def spatial_gating(zu, zv, w_s, b_s, ln_g, ln_b):
    bsz, s, _ = zu.shape
    u = jax.nn.gelu(zu, approximate=False)
    v = layer_norm(jax.nn.gelu(zv, approximate=False), ln_g, ln_b)
    vr = v.reshape(bsz, s // SG_CHUNK, SG_CHUNK, SG_GROUPS, SG_GROUP_DIM)
    sg = jnp.einsum('gts,bcsge->bctge', w_s, vr) + b_s.T[None, None, :, :, None]
    return u * sg.reshape(bsz, s, SG_WIDTH)


def setup_inputs(seed: int = 0) -> dict:
    key = jax.random.key(seed)
    ks = jax.random.split(key, 24)
    f32 = jnp.float32

    def nrm(k, shape, scale):
        return jax.random.normal(k, shape, f32) * scale

    def gain(k, shape):
        return 1.0 + 0.05 * jax.random.normal(k, shape, f32)

    return {
        "x": nrm(ks[0], (BATCH, SEQ, D_MODEL), 1.0),
        "p": nrm(ks[1], (DEPTH, BATCH, SEQ, PLE_DIM), 1.0),
        "norm_mix_pre": gain(ks[2], (DEPTH, D_MODEL)),
        "w_in": nrm(ks[3], (DEPTH, D_MODEL, N_IN), D_MODEL ** -0.5),
        "lb_gamma_fwd": nrm(ks[4], (DEPTH, HG_KEY), 0.1),
        "lb_gamma_bwd": nrm(ks[5], (DEPTH, HG_KEY), 0.1),
        "hg_norm": gain(ks[6], (DEPTH, HG_VAL)),
        "sg_w": nrm(ks[7], (DEPTH, SG_GROUPS, SG_CHUNK, SG_CHUNK), SG_CHUNK ** -0.5),
        "sg_b": gain(ks[8], (DEPTH, SG_GROUPS, SG_CHUNK)),
        "sg_ln_g": gain(ks[9], (DEPTH, SG_WIDTH)),
        "sg_ln_b": nrm(ks[10], (DEPTH, SG_WIDTH), 0.02),
        "w_a": nrm(ks[11], (DEPTH, HG_VAL, D_MODEL), HG_VAL ** -0.5),
        "w_b": nrm(ks[12], (DEPTH, SG_WIDTH, D_MODEL), SG_WIDTH ** -0.5),
        "w_out": nrm(ks[13], (DEPTH, D_MODEL, D_MODEL), D_MODEL ** -0.5),
        "norm_mix_post": gain(ks[14], (DEPTH, D_MODEL)),
        "norm_ffn_pre": gain(ks[15], (DEPTH, D_MODEL)),
        "w_gate": nrm(ks[16], (DEPTH, D_MODEL, FFN_HIDDEN), D_MODEL ** -0.5),
        "w_up": nrm(ks[17], (DEPTH, D_MODEL, FFN_HIDDEN), D_MODEL ** -0.5),
        "w_down": nrm(ks[18], (DEPTH, FFN_HIDDEN, D_MODEL), FFN_HIDDEN ** -0.5),
        "norm_ffn_post": gain(ks[19], (DEPTH, D_MODEL)),
        "w_ple": nrm(ks[20], (DEPTH, PLE_DIM, D_MODEL), PLE_DIM ** -0.5),
        "w_ple_gate": nrm(ks[21], (DEPTH, D_MODEL, D_MODEL), D_MODEL ** -0.5),
    }


def reference(x, p, norm_mix_pre, w_in, lb_gamma_fwd, lb_gamma_bwd, hg_norm, sg_w, sg_b,
              sg_ln_g, sg_ln_b, w_a, w_b, w_out, norm_mix_post, norm_ffn_pre, w_gate, w_up,
              w_down, norm_ffn_post, w_ple, w_ple_gate):
    lb_fwd_all = layer_lower_bounds(lb_gamma_fwd)
    lb_bwd_all = layer_lower_bounds(lb_gamma_bwd)
    offsets = [int(o) for o in np.cumsum(IN_SPLITS)[:-1]]
    for l in range(DEPTH):
        h = rms_norm(x, norm_mix_pre[l])
        z = jnp.einsum('bsd,dn->bsn', h, w_in[l])
        zq, zf_f, zf_b, zi, zg, zu, zv, ga, gb = jnp.split(z, offsets, axis=-1)
        a_out = hgrn2_mixer(zq, zf_f, zf_b, zi, zg, lb_fwd_all[l], lb_bwd_all[l], hg_norm[l])
        b_out = spatial_gating(zu, zv, sg_w[l], sg_b[l], sg_ln_g[l], sg_ln_b[l])
        merged = (jax.nn.sigmoid(ga) * jnp.einsum('bsv,vd->bsd', a_out, w_a[l])
                  + jax.nn.sigmoid(gb) * jnp.einsum('bsw,wd->bsd', b_out, w_b[l]))
        mix = jnp.einsum('bsd,de->bse', merged, w_out[l])
        x = x + rms_norm(mix, norm_mix_post[l])
        h2 = rms_norm(x, norm_ffn_pre[l])
        ff = jnp.einsum('bsf,fd->bsd',
                        jax.nn.silu(jnp.einsum('bsd,df->bsf', h2, w_gate[l]))
                        * jnp.einsum('bsd,df->bsf', h2, w_up[l]), w_down[l])
        x = x + rms_norm(ff, norm_ffn_post[l])
        x = x + (jnp.einsum('bse,ed->bsd', p[l], w_ple[l])
                 * jax.nn.sigmoid(jnp.einsum('bsd,de->bse', x, w_ple_gate[l])))
    return x
```

```python
import functools

import jax
import jax.numpy as jnp
from jax import lax
from jax.experimental import pallas as pl
from jax.experimental.pallas import tpu as pltpu

F32 = jnp.float32
BF16 = jnp.bfloat16

D_MODEL = 1024
HG_HEADS = 8
HG_KEY_DIM = 128
HG_VAL_DIM = 128
HG_KEY = HG_HEADS * HG_KEY_DIM
HG_VAL = HG_HEADS * HG_VAL_DIM
SG_GROUPS = 8
SG_GROUP_DIM = 64
SG_WIDTH = SG_GROUPS * SG_GROUP_DIM
SG_CHUNK = 128
EPS = 1e-6

MXU_WIDTH = 256
LANES = 128
SUBLANES = 8
CHUNK = 128
N_LEVELS = CHUNK.bit_length() - 1
HEADS_PER_PAIR = MXU_WIDTH // HG_KEY_DIM
N_PAIRS = HG_HEADS // HEADS_PER_PAIR

MIX_BLOCK = 256
FFN_BLOCK = 512
VMEM_LIMIT = 52 * 1024 * 1024


def _dot(a, b):
    return jnp.dot(a, b, preferred_element_type=F32)


def _dot_nt(a, b):
    return lax.dot_general(a, b, (((1,), (1,)), ((), ())), preferred_element_type=F32)


def _dot_tn(a, b):
    return lax.dot_general(a, b, (((0,), (0,)), ((), ())), preferred_element_type=F32)


def _rms_norm(x, g):
    return x * lax.rsqrt(jnp.mean(x * x, axis=-1, keepdims=True) + EPS) * g


def _sigmoid(x):
    return 1.0 / (1.0 + jnp.exp(-x))


def _silu(x):
    return x * _sigmoid(x)


def _gelu(x):
    return 0.5 * x * (1.0 + lax.erf(x * (2.0 ** -0.5)))


def _lower_bound(gamma, layer):
    e = jnp.exp(gamma - jnp.max(gamma, axis=0, keepdims=True))
    sm = e / jnp.sum(e, axis=0, keepdims=True)
    lb = jnp.zeros((1, gamma.shape[1]), F32)
    for j in range(1, layer + 1):
        lb = lb + sm[j:j + 1, :]
    return lb


def _forget_gates(zf, lb):
    f = lb + (1.0 - lb) * _sigmoid(zf)
    logf = jnp.log(jnp.maximum(f, jnp.finfo(F32).tiny))
    k = (1.0 - lb) * _sigmoid(-zf)
    return k, logf


def _split3(x):
    hi = x.astype(BF16)
    r = x - hi.astype(F32)
    mid = r.astype(BF16)
    lo = (r - mid.astype(F32)).astype(BF16)
    return hi, mid, lo


def _boundary_rows(b, lg, reverse, row_idx):
    m = 1 << lg
    blk = 2 * m
    r = m if reverse else m - 1
    n, w = b.shape
    if blk >= SUBLANES:
        rows = b.reshape(n // blk, blk, w)[:, r:r + 1, :]
        return jnp.broadcast_to(rows, (n // blk, blk, w)).reshape(n, w)
    pos = row_idx & (blk - 1)
    out = b
    for tm in range(blk):
        if tm != r:
            out = jnp.where(pos == tm, pltpu.roll(b, (tm - r) % n, 0), out)
    return out


def _chunk_masks(reverse):
    t = lax.broadcasted_iota(jnp.int32, (CHUNK, MXU_WIDTH), 0)
    s = lax.broadcasted_iota(jnp.int32, (CHUNK, MXU_WIDTH), 1) & (CHUNK - 1)
    x = t ^ s
    order = (t < s) if reverse else (t > s)
    levels = [jnp.logical_and((x >> lg) == 1, order) for lg in range(N_LEVELS)]
    return levels, t == s


def _hgrn2_chunk(q, k, logf, v, state_ref, reverse, masks):
    level_masks, diag_mask = masks
    row = lax.broadcasted_iota(jnp.int32, (CHUNK, CHUNK), 0)
    col = lax.broadcasted_iota(jnp.int32, (CHUNK, CHUNK), 1)
    tri = ((row <= col) if reverse else (row >= col)).astype(BF16)
    hi, mid, lo = _split3(logf)
    b_all = _dot(tri, hi) + _dot(tri, mid) + _dot(tri, lo)

    lane = lax.broadcasted_iota(jnp.int32, (CHUNK, MXU_WIDTH), 1)
    first_head = lane < HG_KEY_DIM
    row_idx = lax.broadcasted_iota(jnp.int32, (CHUNK, MXU_WIDTH), 0)
    r2 = lax.broadcasted_iota(jnp.int32, (MXU_WIDTH, MXU_WIDTH), 0)
    c2 = lax.broadcasted_iota(jnp.int32, (MXU_WIDTH, MXU_WIDTH), 1)
    same_head = (r2 < HG_VAL_DIM) == (c2 < HG_KEY_DIM)
    edge = 0 if reverse else CHUNK - 1

    outs = []
    for p in range(N_PAIRS):
        sl = slice(p * MXU_WIDTH, (p + 1) * MXU_WIDTH)
        qp, kp, bp, vp = q[:, sl], k[:, sl], b_all[:, sl], v[:, sl]
        k_first = jnp.where(first_head, kp, 0.0)
        k_second = jnp.where(first_head, 0.0, kp)

        acc = jnp.zeros((CHUNK, MXU_WIDTH), F32)
        for lg in range(N_LEVELS):
            w = jnp.exp(-jnp.abs(bp - _boundary_rows(bp, lg, reverse, row_idx)))
            k_bd = jnp.concatenate([(k_first * w).astype(BF16), (k_second * w).astype(BF16)], axis=0)
            s_lv = _dot_nt((qp * w).astype(BF16), k_bd)
            acc = jnp.where(level_masks[lg], s_lv, acc)
        qk = qp * kp
        d_first = jnp.sum(qk[:, :HG_KEY_DIM], axis=1, keepdims=True)
        d_second = jnp.sum(qk[:, HG_KEY_DIM:], axis=1, keepdims=True)
        acc = jnp.where(diag_mask, jnp.where(first_head, d_first, d_second), acc)

        zero_v = jnp.zeros_like(vp)
        v_bd = jnp.concatenate([jnp.where(first_head, vp, zero_v), jnp.where(first_head, zero_v, vp)], axis=0)
        o_intra = _dot(acc.astype(BF16), v_bd)

        st = state_ref[p]
        o_inter = _dot_nt((qp * jnp.exp(bp)).astype(BF16), st.astype(BF16))
        outs.append(o_intra + o_inter)

        b_edge = bp[edge:edge + 1, :]
        k_to_edge = (kp * jnp.exp(b_edge - bp)).astype(BF16)
        upd = _dot_tn(vp, k_to_edge)
        state_ref[p] = st * jnp.exp(b_edge) + jnp.where(same_head, upd, 0.0)
    return jnp.concatenate(outs, axis=1)


def _bwd_pass_kernel(x_ref, g_pre_ref, w_ref, gamma_ref, q_ref, v_ref, o_ref, state_ref, *, layer):
    @pl.when(pl.program_id(1) == 0)
    def _():
        state_ref[...] = jnp.zeros_like(state_ref)

    h = _rms_norm(x_ref[0], g_pre_ref[...]).astype(BF16)
    q = _silu(_dot(h, w_ref[:, 0:HG_KEY])).astype(BF16)
    q_ref[0] = q
    v = _dot(h, w_ref[:, 2 * HG_KEY:2 * HG_KEY + HG_VAL]).astype(BF16)
    v_ref[0] = v
    k, logf = _forget_gates(_dot(h, w_ref[:, HG_KEY:2 * HG_KEY]), _lower_bound(gamma_ref[...], layer))
    qf = q.astype(F32)
    masks = _chunk_masks(True)
    for c in reversed(range(MIX_BLOCK // CHUNK)):
        rows = slice(c * CHUNK, (c + 1) * CHUNK)
        o_ref[0, rows, :] = _hgrn2_chunk(qf[rows], k[rows], logf[rows], v[rows], state_ref, True, masks)


def _fwd_pass_kernel(x_ref, q_ref, v_ref, ob_ref, g_pre_ref, w_ref, gamma_ref, hg_norm_ref, sg_w_ref,
                     sg_bias_ref, ln_g_ref, ln_b_ref, w_a_ref, w_b_ref, w_out_ref, g_post_ref,
                     out_ref, state_ref, *, layer):
    @pl.when(pl.program_id(1) == 0)
    def _():
        state_ref[...] = jnp.zeros_like(state_ref)

    x = x_ref[0]
    h = _rms_norm(x, g_pre_ref[...]).astype(BF16)
    c_f, c_g, c_u, c_v, c_ga, c_gb = 0, HG_KEY, HG_KEY + HG_VAL, HG_KEY + HG_VAL + SG_WIDTH, \
        HG_KEY + HG_VAL + 2 * SG_WIDTH, HG_KEY + HG_VAL + 2 * SG_WIDTH + D_MODEL

    k, logf = _forget_gates(_dot(h, w_ref[:, c_f:c_g]), _lower_bound(gamma_ref[...], layer))
    qf = q_ref[0].astype(F32)
    v = v_ref[0]
    masks = _chunk_masks(False)
    o_chunks = []
    for c in range(MIX_BLOCK // CHUNK):
        rows = slice(c * CHUNK, (c + 1) * CHUNK)
        o_chunks.append(_hgrn2_chunk(qf[rows], k[rows], logf[rows], v[rows], state_ref, False, masks))
    o = jnp.concatenate(o_chunks, axis=0) + ob_ref[0]
    heads = []
    for hd in range(HG_HEADS):
        oh = o[:, hd * HG_VAL_DIM:(hd + 1) * HG_VAL_DIM]
        heads.append(oh * lax.rsqrt(jnp.mean(oh * oh, axis=-1, keepdims=True) + EPS))
    a_out = jnp.concatenate(heads, axis=1) * hg_norm_ref[...] * _silu(_dot(h, w_ref[:, c_g:c_u]))
    a_proj = _dot(a_out.astype(BF16), w_a_ref[...])

    u = _gelu(_dot(h, w_ref[:, c_u:c_v]))
    gv = _gelu(_dot(h, w_ref[:, c_v:c_ga]))
    mu = jnp.mean(gv, axis=-1, keepdims=True)
    gc = gv - mu
    vn = (gc * lax.rsqrt(jnp.mean(gc * gc, axis=-1, keepdims=True) + EPS) * ln_g_ref[...] + ln_b_ref[...]).astype(BF16)
    lane = lax.broadcasted_iota(jnp.int32, (SG_CHUNK, LANES), 1)
    first_group = lane < SG_GROUP_DIM
    sg_rows = []
    for c in range(MIX_BLOCK // SG_CHUNK):
        rows = slice(c * SG_CHUNK, (c + 1) * SG_CHUNK)
        cols = []
        for gp in range(SG_WIDTH // LANES):
            vp = vn[rows, gp * LANES:(gp + 1) * LANES]
            zero_v = jnp.zeros_like(vp)
            v_bd = jnp.concatenate([jnp.where(first_group, vp, zero_v), jnp.where(first_group, zero_v, vp)], axis=0)
            cols.append(_dot(sg_w_ref[gp], v_bd))
        sg_rows.append(jnp.concatenate(cols, axis=1) + sg_bias_ref[...])
    b_out = u * jnp.concatenate(sg_rows, axis=0)
    b_proj = _dot(b_out.astype(BF16), w_b_ref[...])

    merged = (_sigmoid(_dot(h, w_ref[:, c_ga:c_gb])) * a_proj
              + _sigmoid(_dot(h, w_ref[:, c_gb:c_gb + D_MODEL])) * b_proj)
    mix = _dot(merged.astype(BF16), w_out_ref[...])
    out_ref[0] = x + _rms_norm(mix, g_post_ref[...])


def _ffn_pass_kernel(x_ref, p_ref, g_pre_ref, w_gate_ref, w_up_ref, w_down_ref, g_post_ref,
                     w_ple_ref, w_ple_gate_ref, out_ref):
    x = x_ref[...]
    h = _rms_norm(x, g_pre_ref[...]).astype(BF16)
    act = _silu(_dot(h, w_gate_ref[...])) * _dot(h, w_up_ref[...])
    ff = _dot(act.astype(BF16), w_down_ref[...])
    x = x + _rms_norm(ff, g_post_ref[...])
    emb = _dot(p_ref[...].astype(BF16), w_ple_ref[...])
    out_ref[...] = x + emb * _sigmoid(_dot(x.astype(BF16), w_ple_gate_ref[...]))


def _resident(shape):
    return pl.BlockSpec(shape, lambda *_: (0,) * len(shape), pipeline_mode=pl.Buffered(1))


def _row(a):
    return a.reshape(1, -1).astype(F32)


def _mixer_layer(x, layer, norm_pre, w_in, gamma_f, gamma_b, hg_norm, sg_w, sg_b, ln_g, ln_b, w_a, w_b,
                 w_out, norm_post):
    bsz, seq, d = x.shape
    nblk = seq // MIX_BLOCK
    state = pltpu.VMEM((N_PAIRS, MXU_WIDTH, MXU_WIDTH), F32)
    params = pltpu.CompilerParams(dimension_semantics=("arbitrary", "arbitrary"), vmem_limit_bytes=VMEM_LIMIT)

    o_ff, o_fb, o_i, o_g = HG_KEY, 2 * HG_KEY, 3 * HG_KEY, 3 * HG_KEY + HG_VAL
    w_bwd = jnp.concatenate([w_in[:, :o_ff], w_in[:, o_fb:o_i], w_in[:, o_i:o_g]], axis=1).astype(BF16)
    w_fwd = jnp.concatenate([w_in[:, o_ff:o_fb], w_in[:, o_g:]], axis=1).astype(BF16)

    tok = lambda width, rev: pl.BlockSpec(
        (1, MIX_BLOCK, width), (lambda b, i: (b, nblk - 1 - i, 0)) if rev else (lambda b, i: (b, i, 0)))

    q, v, o_bwd = pl.pallas_call(
        functools.partial(_bwd_pass_kernel, layer=layer),
        grid=(bsz, nblk),
        in_specs=[tok(d, True), _resident((1, d)), _resident(w_bwd.shape), _resident(gamma_b.shape)],
        out_specs=[tok(HG_KEY, True), tok(HG_VAL, True), tok(HG_VAL, True)],
        out_shape=[jax.ShapeDtypeStruct((bsz, seq, HG_KEY), BF16),
                   jax.ShapeDtypeStruct((bsz, seq, HG_VAL), BF16),
                   jax.ShapeDtypeStruct((bsz, seq, HG_VAL), F32)],
        scratch_shapes=[state],
        compiler_params=params,
        name=f"hgrn2_bwd_pass_l{layer}",
    )(x, _row(norm_pre), w_bwd, gamma_b)

    sg_w_pairs = sg_w.reshape(SG_GROUPS // 2, 2, SG_CHUNK, SG_CHUNK).transpose(0, 2, 1, 3).reshape(
        SG_GROUPS // 2, SG_CHUNK, 2 * SG_CHUNK).astype(BF16)
    sg_bias = jnp.repeat(sg_b.T, SG_GROUP_DIM, axis=1).astype(F32)
    small = [_row(norm_pre), w_fwd, gamma_f, _row(hg_norm), sg_w_pairs, sg_bias, _row(ln_g), _row(ln_b),
             w_a.astype(BF16), w_b.astype(BF16), w_out.astype(BF16), _row(norm_post)]
    return pl.pallas_call(
        functools.partial(_fwd_pass_kernel, layer=layer),
        grid=(bsz, nblk),
        in_specs=[tok(d, False), tok(HG_KEY, False), tok(HG_VAL, False), tok(HG_VAL, False)]
        + [_resident(a.shape) for a in small],
        out_specs=tok(d, False),
        out_shape=jax.ShapeDtypeStruct((bsz, seq, d), F32),
        scratch_shapes=[state],
        compiler_params=params,
        name=f"hgrn2_fwd_mix_pass_l{layer}",
    )(x, q, v, o_bwd, *small)


def _ffn_layer(x, p, layer, norm_pre, w_gate, w_up, w_down, norm_post, w_ple, w_ple_gate):
    bsz, seq, d = x.shape
    tokens = bsz * seq
    small = [_row(norm_pre), w_gate.astype(BF16), w_up.astype(BF16), w_down.astype(BF16), _row(norm_post),
             w_ple.astype(BF16), w_ple_gate.astype(BF16)]
    tok = lambda width: pl.BlockSpec((FFN_BLOCK, width), lambda i: (i, 0))
    out = pl.pallas_call(
        _ffn_pass_kernel,
        grid=(tokens // FFN_BLOCK,),
        in_specs=[tok(d), tok(p.shape[-1])] + [_resident(a.shape) for a in small],
        out_specs=tok(d),
        out_shape=jax.ShapeDtypeStruct((tokens, d), F32),
        compiler_params=pltpu.CompilerParams(dimension_semantics=("arbitrary",), vmem_limit_bytes=VMEM_LIMIT),
        name=f"swiglu_ple_pass_l{layer}",
    )(x.reshape(tokens, d), p.reshape(tokens, p.shape[-1]), *small)
    return out.reshape(bsz, seq, d)


def kernel(x, p, norm_mix_pre, w_in, lb_gamma_fwd, lb_gamma_bwd, hg_norm, sg_w, sg_b, sg_ln_g, sg_ln_b, w_a, w_b, w_out, norm_mix_post, norm_ffn_pre, w_gate, w_up, w_down, norm_ffn_post, w_ple, w_ple_gate):
    assert x.shape[1] % MIX_BLOCK == 0 and (x.shape[0] * x.shape[1]) % FFN_BLOCK == 0
    for l in range(w_in.shape[0]):
        x = _mixer_layer(x, l, norm_mix_pre[l], w_in[l], lb_gamma_fwd, lb_gamma_bwd, hg_norm[l], sg_w[l],
                         sg_b[l], sg_ln_g[l], sg_ln_b[l], w_a[l], w_b[l], w_out[l], norm_mix_post[l])
        x = _ffn_layer(x, p[l], l, norm_ffn_pre[l], w_gate[l], w_up[l], w_down[l], norm_ffn_post[l],
                       w_ple[l], w_ple_gate[l])
    return x
```

```python
import functools

import jax
import jax.numpy as jnp
from jax import lax
from jax.experimental import pallas as pl
from jax.experimental.pallas import tpu as pltpu

F32 = jnp.float32
BF16 = jnp.bfloat16

D_MODEL = 1024
HG_HEADS = 8
HG_KEY_DIM = 128
HG_VAL_DIM = 128
HG_KEY = HG_HEADS * HG_KEY_DIM
HG_VAL = HG_HEADS * HG_VAL_DIM
SG_GROUPS = 8
SG_GROUP_DIM = 64
SG_WIDTH = SG_GROUPS * SG_GROUP_DIM
SG_CHUNK = 128
EPS = 1e-6
LOG2E = 1.4426950408889634

MXU_WIDTH = 256
LANES = 128
SUBLANES = 8
BF16_ROWS = 16
CHUNK = 128
N_LEVELS = CHUNK.bit_length() - 1
HEADS_PER_PAIR = MXU_WIDTH // HG_KEY_DIM
N_PAIRS = HG_HEADS // HEADS_PER_PAIR

MIX_BLOCK = 256
FFN_BLOCK = 512
VMEM_LIMIT = 52 * 1024 * 1024


def _dot(a, b):
    return jnp.dot(a, b, preferred_element_type=F32)


def _dot_nt(a, b):
    return lax.dot_general(a, b, (((1,), (1,)), ((), ())), preferred_element_type=F32)


def _dot_tn(a, b):
    return lax.dot_general(a, b, (((0,), (0,)), ((), ())), preferred_element_type=F32)


def _rms_norm(x, g):
    return x * lax.rsqrt(jnp.mean(x * x, axis=-1, keepdims=True) + EPS) * g


def _sigmoid(x):
    return 1.0 / (1.0 + jnp.exp(-x))


def _silu(x):
    return x * _sigmoid(x)


def _gelu(x):
    return 0.5 * x * (1.0 + lax.erf(x * (2.0 ** -0.5)))


def _lower_bound(gamma, layer):
    e = jnp.exp(gamma - jnp.max(gamma, axis=0, keepdims=True))
    sm = e / jnp.sum(e, axis=0, keepdims=True)
    lb = jnp.zeros((1, gamma.shape[1]), F32)
    for j in range(1, layer + 1):
        lb = lb + sm[j:j + 1, :]
    return lb


def _forget_gates(zf, lb):
    e = jnp.exp(-jnp.abs(zf))
    big = 1.0 / (1.0 + e)
    small = e * big
    pos = zf >= 0.0
    f = jnp.maximum(lb + (1.0 - lb) * jnp.where(pos, big, small), jnp.finfo(F32).tiny)
    k = (1.0 - lb) * jnp.where(pos, small, big)
    return k, jnp.log(f), f


def _split2(x):
    hi = x.astype(BF16)
    return hi, (x - hi.astype(F32)).astype(BF16)


def _block_diag(x):
    w = x.shape[1] // 2
    z = jnp.zeros((x.shape[0], w), x.dtype)
    return jnp.concatenate([jnp.concatenate([x[:, :w], z], axis=1),
                            jnp.concatenate([z, x[:, w:]], axis=1)], axis=0)


def _block_diag_t(x):
    w = x.shape[1] // 2
    xt = x.T
    z = jnp.zeros((w, x.shape[0]), x.dtype)
    return jnp.concatenate([jnp.concatenate([xt[:w], z], axis=1),
                            jnp.concatenate([z, xt[w:]], axis=1)], axis=0)


def _boundary_rows(b, lg, reverse, pos):
    m = 1 << lg
    blk = 2 * m
    r = m if reverse else m - 1
    n, w = b.shape
    if blk >= SUBLANES:
        rows = b.reshape(n // blk, blk, w)[:, r:r + 1, :]
        return jnp.broadcast_to(rows, (n // blk, blk, w)).reshape(n, w)
    groups = b.reshape(n // SUBLANES, SUBLANES, w)
    out = None
    for j in range(SUBLANES // blk):
        row = j * blk + r
        cand = jnp.broadcast_to(groups[:, row:row + 1, :], groups.shape).reshape(n, w)
        out = cand if out is None else jnp.where((pos & (SUBLANES - 1)) >= j * blk, cand, out)
    return out


def _level_weight(b2, lg, reverse, pos):
    m = 1 << lg
    n, w = b2.shape
    if m < SUBLANES:
        return jnp.exp2(-jnp.abs(b2 - _boundary_rows(b2, lg, reverse, pos)))
    b4 = b2.reshape(n // (2 * m), 2, m, w)
    if reverse:
        mid = b4[:, 1:2, 0:1, :]
        e = jnp.concatenate([b4[:, 0:1] - mid, mid - b4[:, 1:2]], axis=1)
    else:
        mid = b4[:, 0:1, m - 1:m, :]
        e = jnp.concatenate([mid - b4[:, 0:1], b4[:, 1:2] - mid], axis=1)
    return jnp.exp2(e.reshape(n, w))


def _chunk_masks(reverse):
    t = lax.broadcasted_iota(jnp.int32, (CHUNK, MXU_WIDTH), 0)
    s = lax.broadcasted_iota(jnp.int32, (CHUNK, MXU_WIDTH), 1) & (CHUNK - 1)
    x = t ^ s
    order = (t < s) if reverse else (t > s)
    q_bit = 0 if reverse else 1
    is_query = [((t >> lg) & 1) == q_bit for lg in range(N_LEVELS)]
    levels = [jnp.logical_and((x >> lg) == 1, order) for lg in range(N_LEVELS)]
    return is_query, levels, t == s, t


def _merge_level(acc, xw, mask, lg, reverse):
    m = 1 << lg
    if m < BF16_ROWS:
        return jnp.where(mask, _dot(xw, _block_diag_t(xw)), acc)
    starts = [base + (0 if reverse else m) for base in range(0, CHUNK, 2 * m)]
    s_q = _dot(jnp.concatenate([xw[s:s + m] for s in starts], axis=0), _block_diag_t(xw))
    pieces = []
    for i, s in enumerate(starts):
        merged = jnp.where(mask[s:s + m], s_q[i * m:(i + 1) * m], acc[s:s + m])
        other = acc[s + m:s + 2 * m] if reverse else acc[s - m:s]
        pieces += [merged, other] if reverse else [other, merged]
    return jnp.concatenate(pieces, axis=0)


def _hgrn2_chunk(q, k, f, logf, v, state_ref, reverse, masks):
    is_query, level_masks, diag_mask, pos = masks
    row = lax.broadcasted_iota(jnp.int32, (CHUNK, CHUNK), 0)
    col = lax.broadcasted_iota(jnp.int32, (CHUNK, CHUNK), 1)
    tri = ((row <= col) if reverse else (row >= col)).astype(BF16)
    hi, mid = _split2(logf)
    b2_all = (_dot(tri, hi) + _dot(tri, mid)) * LOG2E
    qf = q * f.astype(BF16)
    edge = 0 if reverse else CHUNK - 1

    outs = []
    for p in range(N_PAIRS):
        sl = slice(p * MXU_WIDTH, (p + 1) * MXU_WIDTH)
        qp, kp, b2, vp = q[:, sl], k[:, sl], b2_all[:, sl], v[:, sl]

        acc = jnp.where(diag_mask, _dot(qp, _block_diag_t(kp)), 0.0)
        for lg in range(N_LEVELS):
            if lg == 0:
                xw = jnp.where(is_query[0], qf[:, sl], kp)
            else:
                xw = jnp.where(is_query[lg], qp, kp) * _level_weight(b2, lg, reverse, pos).astype(BF16)
            acc = _merge_level(acc, xw, level_masks[lg], lg, reverse)
        o_intra = _dot(acc.astype(BF16), _block_diag(vp))

        h0 = p * HEADS_PER_PAIR
        st = [state_ref[h0 + j] for j in range(HEADS_PER_PAIR)]
        z = jnp.zeros((HG_VAL_DIM, HG_KEY_DIM), BF16)
        st_bd = jnp.concatenate([jnp.concatenate([st[0].astype(BF16).T, z], axis=1),
                                 jnp.concatenate([z, st[1].astype(BF16).T], axis=1)], axis=0)
        o_inter = _dot(qp * jnp.exp2(b2).astype(BF16), st_bd)
        outs.append(o_intra + o_inter)

        b2_edge = b2[edge:edge + 1, :]
        k_to_edge = kp * jnp.exp2(b2_edge - b2).astype(BF16)
        decay = jnp.exp2(b2_edge)
        for j in range(HEADS_PER_PAIR):
            hs = slice(j * HG_KEY_DIM, (j + 1) * HG_KEY_DIM)
            state_ref[h0 + j] = st[j] * decay[:, hs] + _dot_tn(vp[:, hs], k_to_edge[:, hs])
    return jnp.concatenate(outs, axis=1)


def _bwd_pass_kernel(x_ref, g_pre_ref, w_ref, gamma_ref, q_ref, v_ref, o_ref, state_ref, *, layer):
    @pl.when(pl.program_id(1) == 0)
    def _():
        state_ref[...] = jnp.zeros_like(state_ref)

    h = _rms_norm(x_ref[0], g_pre_ref[...]).astype(BF16)
    q = _silu(_dot(h, w_ref[:, 0:HG_KEY])).astype(BF16)
    q_ref[0] = q
    v = _dot(h, w_ref[:, 2 * HG_KEY:2 * HG_KEY + HG_VAL]).astype(BF16)
    v_ref[0] = v
    k, logf, f = _forget_gates(_dot(h, w_ref[:, HG_KEY:2 * HG_KEY]), _lower_bound(gamma_ref[...], layer))
    k = k.astype(BF16)
    masks = _chunk_masks(True)
    for c in reversed(range(MIX_BLOCK // CHUNK)):
        rows = slice(c * CHUNK, (c + 1) * CHUNK)
        o_ref[0, rows, :] = _hgrn2_chunk(q[rows], k[rows], f[rows], logf[rows], v[rows], state_ref, True, masks)


def _fwd_pass_kernel(x_ref, q_ref, v_ref, ob_ref, g_pre_ref, w_ref, gamma_ref, hg_norm_ref, sg_w_ref,
                     sg_bias_ref, ln_g_ref, ln_b_ref, w_a_ref, w_b_ref, w_out_ref, g_post_ref,
                     out_ref, state_ref, *, layer):
    @pl.when(pl.program_id(1) == 0)
    def _():
        state_ref[...] = jnp.zeros_like(state_ref)

    x = x_ref[0]
    h = _rms_norm(x, g_pre_ref[...]).astype(BF16)
    c_f, c_g, c_u, c_v, c_ga, c_gb = 0, HG_KEY, HG_KEY + HG_VAL, HG_KEY + HG_VAL + SG_WIDTH, \
        HG_KEY + HG_VAL + 2 * SG_WIDTH, HG_KEY + HG_VAL + 2 * SG_WIDTH + D_MODEL

    k, logf, f = _forget_gates(_dot(h, w_ref[:, c_f:c_g]), _lower_bound(gamma_ref[...], layer))
    k = k.astype(BF16)
    q = q_ref[0]
    v = v_ref[0]
    masks = _chunk_masks(False)
    o_chunks = []
    for c in range(MIX_BLOCK // CHUNK):
        rows = slice(c * CHUNK, (c + 1) * CHUNK)
        o_chunks.append(_hgrn2_chunk(q[rows], k[rows], f[rows], logf[rows], v[rows], state_ref, False, masks))
    o = jnp.concatenate(o_chunks, axis=0) + ob_ref[0]
    heads = []
    for hd in range(HG_HEADS):
        oh = o[:, hd * HG_VAL_DIM:(hd + 1) * HG_VAL_DIM]
        heads.append(oh * lax.rsqrt(jnp.mean(oh * oh, axis=-1, keepdims=True) + EPS))
    a_out = jnp.concatenate(heads, axis=1) * hg_norm_ref[...] * _silu(_dot(h, w_ref[:, c_g:c_u]))
    a_proj = _dot(a_out.astype(BF16), w_a_ref[...])

    u = _gelu(_dot(h, w_ref[:, c_u:c_v]))
    gv = _gelu(_dot(h, w_ref[:, c_v:c_ga]))
    mu = jnp.mean(gv, axis=-1, keepdims=True)
    gc = gv - mu
    vn = (gc * lax.rsqrt(jnp.mean(gc * gc, axis=-1, keepdims=True) + EPS) * ln_g_ref[...] + ln_b_ref[...]).astype(BF16)
    sg_rows = []
    for c in range(MIX_BLOCK // SG_CHUNK):
        rows = slice(c * SG_CHUNK, (c + 1) * SG_CHUNK)
        cols = [_dot(sg_w_ref[gp], _block_diag(vn[rows, gp * LANES:(gp + 1) * LANES]))
                for gp in range(SG_WIDTH // LANES)]
        sg_rows.append(jnp.concatenate(cols, axis=1) + sg_bias_ref[...])
    b_out = u * jnp.concatenate(sg_rows, axis=0)
    b_proj = _dot(b_out.astype(BF16), w_b_ref[...])

    merged = (_sigmoid(_dot(h, w_ref[:, c_ga:c_gb])) * a_proj
              + _sigmoid(_dot(h, w_ref[:, c_gb:c_gb + D_MODEL])) * b_proj)
    mix = _dot(merged.astype(BF16), w_out_ref[...])
    out_ref[0] = x + _rms_norm(mix, g_post_ref[...])


def _ffn_pass_kernel(x_ref, p_ref, g_pre_ref, w_gate_ref, w_up_ref, w_down_ref, g_post_ref,
                     w_ple_ref, w_ple_gate_ref, out_ref):
    x = x_ref[...]
    h = _rms_norm(x, g_pre_ref[...]).astype(BF16)
    act = _silu(_dot(h, w_gate_ref[...])) * _dot(h, w_up_ref[...])
    ff = _dot(act.astype(BF16), w_down_ref[...])
    x = x + _rms_norm(ff, g_post_ref[...])
    emb = _dot(p_ref[...].astype(BF16), w_ple_ref[...])
    out_ref[...] = x + emb * _sigmoid(_dot(x.astype(BF16), w_ple_gate_ref[...]))


def _resident(shape):
    return pl.BlockSpec(shape, lambda *_: (0,) * len(shape), pipeline_mode=pl.Buffered(1))


def _row(a):
    return a.reshape(1, -1).astype(F32)


def _mixer_layer(x, layer, norm_pre, w_in, gamma_f, gamma_b, hg_norm, sg_w, sg_b, ln_g, ln_b, w_a, w_b,
                 w_out, norm_post):
    bsz, seq, d = x.shape
    nblk = seq // MIX_BLOCK
    state = pltpu.VMEM((HG_HEADS, HG_VAL_DIM, HG_KEY_DIM), F32)
    params = pltpu.CompilerParams(dimension_semantics=("arbitrary", "arbitrary"), vmem_limit_bytes=VMEM_LIMIT)

    o_ff, o_fb, o_i, o_g = HG_KEY, 2 * HG_KEY, 3 * HG_KEY, 3 * HG_KEY + HG_VAL
    w_bwd = jnp.concatenate([w_in[:, :o_ff], w_in[:, o_fb:o_i], w_in[:, o_i:o_g]], axis=1).astype(BF16)
    w_fwd = jnp.concatenate([w_in[:, o_ff:o_fb], w_in[:, o_g:]], axis=1).astype(BF16)

    tok = lambda width, rev: pl.BlockSpec(
        (1, MIX_BLOCK, width), (lambda b, i: (b, nblk - 1 - i, 0)) if rev else (lambda b, i: (b, i, 0)))

    q, v, o_bwd = pl.pallas_call(
        functools.partial(_bwd_pass_kernel, layer=layer),
        grid=(bsz, nblk),
        in_specs=[tok(d, True), _resident((1, d)), _resident(w_bwd.shape), _resident(gamma_b.shape)],
        out_specs=[tok(HG_KEY, True), tok(HG_VAL, True), tok(HG_VAL, True)],
        out_shape=[jax.ShapeDtypeStruct((bsz, seq, HG_KEY), BF16),
                   jax.ShapeDtypeStruct((bsz, seq, HG_VAL), BF16),
                   jax.ShapeDtypeStruct((bsz, seq, HG_VAL), F32)],
        scratch_shapes=[state],
        compiler_params=params,
        name=f"hgrn2_bwd_pass_l{layer}",
    )(x, _row(norm_pre), w_bwd, gamma_b)

    sg_w_pairs = sg_w.reshape(SG_GROUPS // 2, 2, SG_CHUNK, SG_CHUNK).transpose(0, 2, 1, 3).reshape(
        SG_GROUPS // 2, SG_CHUNK, 2 * SG_CHUNK).astype(BF16)
    sg_bias = jnp.repeat(sg_b.T, SG_GROUP_DIM, axis=1).astype(F32)
    small = [_row(norm_pre), w_fwd, gamma_f, _row(hg_norm), sg_w_pairs, sg_bias, _row(ln_g), _row(ln_b),
             w_a.astype(BF16), w_b.astype(BF16), w_out.astype(BF16), _row(norm_post)]
    return pl.pallas_call(
        functools.partial(_fwd_pass_kernel, layer=layer),
        grid=(bsz, nblk),
        in_specs=[tok(d, False), tok(HG_KEY, False), tok(HG_VAL, False), tok(HG_VAL, False)]
        + [_resident(a.shape) for a in small],
        out_specs=tok(d, False),
        out_shape=jax.ShapeDtypeStruct((bsz, seq, d), F32),
        scratch_shapes=[state],
        compiler_params=params,
        name=f"hgrn2_fwd_mix_pass_l{layer}",
    )(x, q, v, o_bwd, *small)


def _ffn_layer(x, p, layer, norm_pre, w_gate, w_up, w_down, norm_post, w_ple, w_ple_gate):
    bsz, seq, d = x.shape
    tokens = bsz * seq
    small = [_row(norm_pre), w_gate.astype(BF16), w_up.astype(BF16), w_down.astype(BF16), _row(norm_post),
             w_ple.astype(BF16), w_ple_gate.astype(BF16)]
    tok = lambda width: pl.BlockSpec((FFN_BLOCK, width), lambda i: (i, 0))
    out = pl.pallas_call(
        _ffn_pass_kernel,
        grid=(tokens // FFN_BLOCK,),
        in_specs=[tok(d), tok(p.shape[-1])] + [_resident(a.shape) for a in small],
        out_specs=tok(d),
        out_shape=jax.ShapeDtypeStruct((tokens, d), F32),
        compiler_params=pltpu.CompilerParams(dimension_semantics=("arbitrary",), vmem_limit_bytes=VMEM_LIMIT),
        name=f"swiglu_ple_pass_l{layer}",
    )(x.reshape(tokens, d), p.reshape(tokens, p.shape[-1]), *small)
    return out.reshape(bsz, seq, d)


def kernel(x, p, norm_mix_pre, w_in, lb_gamma_fwd, lb_gamma_bwd, hg_norm, sg_w, sg_b, sg_ln_g, sg_ln_b, w_a, w_b, w_out, norm_mix_post, norm_ffn_pre, w_gate, w_up, w_down, norm_ffn_post, w_ple, w_ple_gate):
    assert x.shape[1] % MIX_BLOCK == 0 and (x.shape[0] * x.shape[1]) % FFN_BLOCK == 0
    for l in range(w_in.shape[0]):
        x = _mixer_layer(x, l, norm_mix_pre[l], w_in[l], lb_gamma_fwd, lb_gamma_bwd, hg_norm[l], sg_w[l],
                         sg_b[l], sg_ln_g[l], sg_ln_b[l], w_a[l], w_b[l], w_out[l], norm_mix_post[l])
        x = _ffn_layer(x, p[l], l, norm_ffn_pre[l], w_gate[l], w_up[l], w_down[l], norm_ffn_post[l],
                       w_ple[l], w_ple_gate[l])
    return x
```

```python
import functools

import jax
import jax.numpy as jnp
from jax import lax
from jax.experimental import pallas as pl
from jax.experimental.pallas import tpu as pltpu

F32 = jnp.float32
BF16 = jnp.bfloat16

D_MODEL = 1024
HG_HEADS = 8
HG_KEY_DIM = 128
HG_VAL_DIM = 128
HG_KEY = HG_HEADS * HG_KEY_DIM
HG_VAL = HG_HEADS * HG_VAL_DIM
SG_GROUPS = 8
SG_GROUP_DIM = 64
SG_WIDTH = SG_GROUPS * SG_GROUP_DIM
SG_CHUNK = 128
EPS = 1e-6

MXU_WIDTH = 256
LANES = 128
SUBLANES = 8
BF16_ROWS = 16
CHUNK = 128
N_LEVELS = CHUNK.bit_length() - 1
HEADS_PER_PAIR = MXU_WIDTH // HG_KEY_DIM
N_PAIRS = HG_HEADS // HEADS_PER_PAIR

MIX_BLOCK = 512
PROJ_ROWS = 256
FFN_BLOCK = 512
VMEM_LIMIT = 52 * 1024 * 1024


def _dot(a, b):
    return jnp.dot(a, b, preferred_element_type=F32)


def _dot_tn(a, b):
    return lax.dot_general(a, b, (((0,), (0,)), ((), ())), preferred_element_type=F32)


def _rms_norm(x, g):
    return x * lax.rsqrt(jnp.mean(x * x, axis=-1, keepdims=True) + EPS) * g


def _sigmoid(x):
    return 1.0 / (1.0 + jnp.exp(-x))


def _silu(x):
    return x * _sigmoid(x)


def _gelu(x):
    return 0.5 * x * (1.0 + lax.erf(x * (2.0 ** -0.5)))


def _lower_bound(gamma, layer):
    e = jnp.exp(gamma - jnp.max(gamma, axis=0, keepdims=True))
    sm = e / jnp.sum(e, axis=0, keepdims=True)
    lb = jnp.zeros((1, gamma.shape[1]), F32)
    for j in range(1, layer + 1):
        lb = lb + sm[j:j + 1, :]
    return lb


def _forget_gates(zf, lb):
    f_raw = lb + (1.0 - lb) * _sigmoid(zf)
    f = jnp.maximum(f_raw, jnp.finfo(F32).tiny)
    return 1.0 - f_raw, f, jnp.log2(f)


def _split2(x):
    hi = x.astype(BF16)
    return hi, (x - hi.astype(F32)).astype(BF16)


def _block_diag(x):
    w = x.shape[1] // 2
    z = jnp.zeros((x.shape[0], w), x.dtype)
    return jnp.concatenate([jnp.concatenate([x[:, :w], z], axis=1),
                            jnp.concatenate([z, x[:, w:]], axis=1)], axis=0)


def _block_diag_t(x):
    w = x.shape[1] // 2
    xt = x.T
    z = jnp.zeros((w, x.shape[0]), x.dtype)
    return jnp.concatenate([jnp.concatenate([xt[:w], z], axis=1),
                            jnp.concatenate([z, xt[w:]], axis=1)], axis=0)


def _boundary_rows(b, lg, reverse, pos):
    m = 1 << lg
    blk = 2 * m
    r = m if reverse else m - 1
    n, w = b.shape
    if blk >= SUBLANES:
        rows = b.reshape(n // blk, blk, w)[:, r:r + 1, :]
        return jnp.broadcast_to(rows, (n // blk, blk, w)).reshape(n, w)
    groups = b.reshape(n // SUBLANES, SUBLANES, w)
    out = None
    for j in range(SUBLANES // blk):
        row = j * blk + r
        cand = jnp.broadcast_to(groups[:, row:row + 1, :], groups.shape).reshape(n, w)
        out = cand if out is None else jnp.where((pos & (SUBLANES - 1)) >= j * blk, cand, out)
    return out


def _level_weight(b2, lg, reverse, pos):
    m = 1 << lg
    n, w = b2.shape
    if m < SUBLANES:
        return jnp.exp2(-jnp.abs(b2 - _boundary_rows(b2, lg, reverse, pos)))
    b4 = b2.reshape(n // (2 * m), 2, m, w)
    if reverse:
        mid = b4[:, 1:2, 0:1, :]
        e = jnp.concatenate([b4[:, 0:1] - mid, mid - b4[:, 1:2]], axis=1)
    else:
        mid = b4[:, 0:1, m - 1:m, :]
        e = jnp.concatenate([mid - b4[:, 0:1], b4[:, 1:2] - mid], axis=1)
    return jnp.exp2(e.reshape(n, w))


def _query_starts(lg, reverse):
    m = 1 << lg
    return [base + (0 if reverse else m) for base in range(0, CHUNK, 2 * m)]


def _chunk_masks(reverse):
    t = lax.broadcasted_iota(jnp.int32, (CHUNK, MXU_WIDTH), 0)
    s = lax.broadcasted_iota(jnp.int32, (CHUNK, MXU_WIDTH), 1) & (CHUNK - 1)
    x = t ^ s
    order = (t < s) if reverse else (t > s)
    entry = [jnp.logical_and((x >> lg) == 1, order) for lg in range(N_LEVELS) if (1 << lg) < SUBLANES]
    s_tile = lax.broadcasted_iota(jnp.int32, (SUBLANES, MXU_WIDTH), 1) & (CHUNK - 1)
    key_half = {lg: s_tile >> lg for lg in range(N_LEVELS) if (1 << lg) >= SUBLANES}
    return entry, t == s, key_half, t


def _level_operands(qp, kp, w, lg, reverse):
    m = 1 << lg
    if m < BF16_ROWS:
        return qp * w, kp * w
    pieces = []
    for s in _query_starts(lg, reverse):
        other = slice(s + m, s + 2 * m) if reverse else slice(s - m, s)
        pieces += [qp[s:s + m], kp[other]] if reverse else [kp[other], qp[s:s + m]]
    xw = jnp.concatenate(pieces, axis=0) * w
    return xw, xw


def _merge_level(acc, xq, xk, lg, reverse, entry, key_half):
    m = 1 << lg
    if m < SUBLANES:
        return jnp.where(entry[lg], _dot(xq, _block_diag_t(xk)), acc)
    starts = _query_starts(lg, reverse)
    if m >= BF16_ROWS:
        s_q = _dot(jnp.concatenate([xq[s:s + m] for s in starts], axis=0), _block_diag_t(xk))
        blocks = [s_q[i * m:(i + 1) * m] for i in range(len(starts))]
    else:
        s_all = _dot(xq, _block_diag_t(xk))
        blocks = [s_all[s:s + m] for s in starts]
    pieces = []
    for s, blk in zip(starts, blocks):
        keys = (s >> lg) + (1 if reverse else -1)
        tiles = (m // SUBLANES, SUBLANES, MXU_WIDTH)
        merged = jnp.where((key_half[lg] == keys)[None], blk.reshape(tiles),
                           acc[s:s + m].reshape(tiles)).reshape(m, MXU_WIDTH)
        other = acc[s + m:s + 2 * m] if reverse else acc[s - m:s]
        pieces += [merged, other] if reverse else [other, merged]
    return jnp.concatenate(pieces, axis=0)


def _hgrn2_chunk(q, k, f, log2f, v, state_ref, reverse, masks):
    entry, diag_mask, key_half, pos = masks
    row = lax.broadcasted_iota(jnp.int32, (CHUNK, CHUNK), 0)
    col = lax.broadcasted_iota(jnp.int32, (CHUNK, CHUNK), 1)
    tri = ((row <= col) if reverse else (row >= col)).astype(BF16)
    hi, mid = _split2(log2f)
    b2_all = _dot(tri, hi) + _dot(tri, mid)
    qf = q * f.astype(BF16)
    edge = 0 if reverse else CHUNK - 1

    outs = []
    for p in range(N_PAIRS):
        sl = slice(p * MXU_WIDTH, (p + 1) * MXU_WIDTH)
        qp, kp, b2, vp = q[:, sl], k[:, sl], b2_all[:, sl], v[:, sl]

        acc = jnp.where(diag_mask, _dot(qp, _block_diag_t(kp)), 0.0)
        for lg in range(N_LEVELS):
            if lg == 0:
                xq, xk = qf[:, sl], kp
            else:
                xq, xk = _level_operands(qp, kp, _level_weight(b2, lg, reverse, pos).astype(BF16), lg, reverse)
            acc = _merge_level(acc, xq, xk, lg, reverse, entry, key_half)
        o_intra = _dot(acc.astype(BF16), _block_diag(vp))

        h0 = p * HEADS_PER_PAIR
        st = [state_ref[h0 + j] for j in range(HEADS_PER_PAIR)]
        z = jnp.zeros((HG_VAL_DIM, HG_KEY_DIM), BF16)
        st_bd = jnp.concatenate([jnp.concatenate([st[0].astype(BF16).T, z], axis=1),
                                 jnp.concatenate([z, st[1].astype(BF16).T], axis=1)], axis=0)
        o_inter = _dot(qp * jnp.exp2(b2).astype(BF16), st_bd)
        outs.append(o_intra + o_inter)

        b2_edge = b2[edge:edge + 1, :]
        k_to_edge = kp * jnp.exp2(b2_edge - b2).astype(BF16)
        decay = jnp.exp2(b2_edge)
        for j in range(HEADS_PER_PAIR):
            hs = slice(j * HG_KEY_DIM, (j + 1) * HG_KEY_DIM)
            state_ref[h0 + j] = st[j] * decay[:, hs] + _dot_tn(vp[:, hs], k_to_edge[:, hs])
    return jnp.concatenate(outs, axis=1)


def _bwd_pass_kernel(x_ref, g_pre_ref, w_ref, gamma_ref, q_ref, v_ref, o_ref, state_ref, *, layer):
    @pl.when(pl.program_id(1) == 0)
    def _():
        state_ref[...] = jnp.zeros_like(state_ref)

    masks = _chunk_masks(True)
    lb = _lower_bound(gamma_ref[...], layer)
    for sb in reversed(range(MIX_BLOCK // PROJ_ROWS)):
        base = sb * PROJ_ROWS
        h = _rms_norm(x_ref[0, base:base + PROJ_ROWS, :], g_pre_ref[...]).astype(BF16)
        q = _silu(_dot(h, w_ref[:, 0:HG_KEY])).astype(BF16)
        q_ref[0, base:base + PROJ_ROWS, :] = q
        v = _dot(h, w_ref[:, 2 * HG_KEY:2 * HG_KEY + HG_VAL]).astype(BF16)
        v_ref[0, base:base + PROJ_ROWS, :] = v
        k, f, log2f = _forget_gates(_dot(h, w_ref[:, HG_KEY:2 * HG_KEY]), lb)
        k = k.astype(BF16)
        for c in reversed(range(PROJ_ROWS // CHUNK)):
            rows = slice(c * CHUNK, (c + 1) * CHUNK)
            o_ref[0, base + c * CHUNK:base + (c + 1) * CHUNK, :] = _hgrn2_chunk(
                q[rows], k[rows], f[rows], log2f[rows], v[rows], state_ref, True, masks)


def _fwd_pass_kernel(x_ref, q_ref, v_ref, ob_ref, g_pre_ref, w_ref, gamma_ref, hg_norm_ref, sg_w_ref,
                     sg_bias_ref, ln_g_ref, ln_b_ref, w_a_ref, w_b_ref, w_out_ref, g_post_ref,
                     out_ref, state_ref, *, layer):
    @pl.when(pl.program_id(1) == 0)
    def _():
        state_ref[...] = jnp.zeros_like(state_ref)

    c_f, c_g, c_u, c_v, c_ga, c_gb = 0, HG_KEY, HG_KEY + HG_VAL, HG_KEY + HG_VAL + SG_WIDTH, \
        HG_KEY + HG_VAL + 2 * SG_WIDTH, HG_KEY + HG_VAL + 2 * SG_WIDTH + D_MODEL
    masks = _chunk_masks(False)
    lb = _lower_bound(gamma_ref[...], layer)

    for sb in range(MIX_BLOCK // PROJ_ROWS):
        blk = slice(sb * PROJ_ROWS, (sb + 1) * PROJ_ROWS)
        x = x_ref[0, blk, :]
        h = _rms_norm(x, g_pre_ref[...]).astype(BF16)

        k, f, log2f = _forget_gates(_dot(h, w_ref[:, c_f:c_g]), lb)
        k = k.astype(BF16)
        q = q_ref[0, blk, :]
        v = v_ref[0, blk, :]
        o_chunks = []
        for c in range(PROJ_ROWS // CHUNK):
            rows = slice(c * CHUNK, (c + 1) * CHUNK)
            o_chunks.append(_hgrn2_chunk(q[rows], k[rows], f[rows], log2f[rows], v[rows], state_ref, False, masks))
        o = jnp.concatenate(o_chunks, axis=0) + ob_ref[0, blk, :]
        heads = []
        for hd in range(HG_HEADS):
            oh = o[:, hd * HG_VAL_DIM:(hd + 1) * HG_VAL_DIM]
            heads.append(oh * lax.rsqrt(jnp.mean(oh * oh, axis=-1, keepdims=True) + EPS))
        a_out = jnp.concatenate(heads, axis=1) * hg_norm_ref[...] * _silu(_dot(h, w_ref[:, c_g:c_u]))
        a_proj = _dot(a_out.astype(BF16), w_a_ref[...])

        u = _gelu(_dot(h, w_ref[:, c_u:c_v]))
        gv = _gelu(_dot(h, w_ref[:, c_v:c_ga]))
        mu = jnp.mean(gv, axis=-1, keepdims=True)
        gc = gv - mu
        vn = (gc * lax.rsqrt(jnp.mean(gc * gc, axis=-1, keepdims=True) + EPS) * ln_g_ref[...]
              + ln_b_ref[...]).astype(BF16)
        sg_rows = []
        for c in range(PROJ_ROWS // SG_CHUNK):
            rows = slice(c * SG_CHUNK, (c + 1) * SG_CHUNK)
            cols = [_dot(sg_w_ref[gp], _block_diag(vn[rows, gp * LANES:(gp + 1) * LANES]))
                    for gp in range(SG_WIDTH // LANES)]
            sg_rows.append(jnp.concatenate(cols, axis=1) + sg_bias_ref[...])
        b_out = u * jnp.concatenate(sg_rows, axis=0)
        b_proj = _dot(b_out.astype(BF16), w_b_ref[...])

        merged = (_sigmoid(_dot(h, w_ref[:, c_ga:c_gb])) * a_proj
                  + _sigmoid(_dot(h, w_ref[:, c_gb:c_gb + D_MODEL])) * b_proj)
        mix = _dot(merged.astype(BF16), w_out_ref[...])
        out_ref[0, blk, :] = x + _rms_norm(mix, g_post_ref[...])


def _ffn_pass_kernel(x_ref, p_ref, g_pre_ref, w_gate_ref, w_up_ref, w_down_ref, g_post_ref,
                     w_ple_ref, w_ple_gate_ref, out_ref):
    x = x_ref[...]
    h = _rms_norm(x, g_pre_ref[...]).astype(BF16)
    act = _silu(_dot(h, w_gate_ref[...])) * _dot(h, w_up_ref[...])
    ff = _dot(act.astype(BF16), w_down_ref[...])
    x = x + _rms_norm(ff, g_post_ref[...])
    emb = _dot(p_ref[...].astype(BF16), w_ple_ref[...])
    out_ref[...] = x + emb * _sigmoid(_dot(x.astype(BF16), w_ple_gate_ref[...]))


def _resident(shape):
    return pl.BlockSpec(shape, lambda *_: (0,) * len(shape), pipeline_mode=pl.Buffered(1))


def _row(a):
    return a.reshape(1, -1).astype(F32)


def _mixer_layer(x, layer, norm_pre, w_in, gamma_f, gamma_b, hg_norm, sg_w, sg_b, ln_g, ln_b, w_a, w_b,
                 w_out, norm_post):
    bsz, seq, d = x.shape
    nblk = seq // MIX_BLOCK
    scratch = [pltpu.VMEM((HG_HEADS, HG_VAL_DIM, HG_KEY_DIM), F32)]
    params = pltpu.CompilerParams(dimension_semantics=("arbitrary", "arbitrary"), vmem_limit_bytes=VMEM_LIMIT)

    o_ff, o_fb, o_i, o_g = HG_KEY, 2 * HG_KEY, 3 * HG_KEY, 3 * HG_KEY + HG_VAL
    w_bwd = jnp.concatenate([w_in[:, :o_ff], w_in[:, o_fb:o_i], w_in[:, o_i:o_g]], axis=1).astype(BF16)
    w_fwd = jnp.concatenate([w_in[:, o_ff:o_fb], w_in[:, o_g:]], axis=1).astype(BF16)

    tok = lambda width, rev: pl.BlockSpec(
        (1, MIX_BLOCK, width), (lambda b, i: (b, nblk - 1 - i, 0)) if rev else (lambda b, i: (b, i, 0)))

    q, v, o_bwd = pl.pallas_call(
        functools.partial(_bwd_pass_kernel, layer=layer),
        grid=(bsz, nblk),
        in_specs=[tok(d, True), _resident((1, d)), _resident(w_bwd.shape), _resident(gamma_b.shape)],
        out_specs=[tok(HG_KEY, True), tok(HG_VAL, True), tok(HG_VAL, True)],
        out_shape=[jax.ShapeDtypeStruct((bsz, seq, HG_KEY), BF16),
                   jax.ShapeDtypeStruct((bsz, seq, HG_VAL), BF16),
                   jax.ShapeDtypeStruct((bsz, seq, HG_VAL), F32)],
        scratch_shapes=scratch,
        compiler_params=params,
        name=f"hgrn2_bwd_pass_l{layer}",
    )(x, _row(norm_pre), w_bwd, gamma_b)

    sg_w_pairs = sg_w.reshape(SG_GROUPS // 2, 2, SG_CHUNK, SG_CHUNK).transpose(0, 2, 1, 3).reshape(
        SG_GROUPS // 2, SG_CHUNK, 2 * SG_CHUNK).astype(BF16)
    sg_bias = jnp.repeat(sg_b.T, SG_GROUP_DIM, axis=1).astype(F32)
    small = [_row(norm_pre), w_fwd, gamma_f, _row(hg_norm), sg_w_pairs, sg_bias, _row(ln_g), _row(ln_b),
             w_a.astype(BF16), w_b.astype(BF16), w_out.astype(BF16), _row(norm_post)]
    return pl.pallas_call(
        functools.partial(_fwd_pass_kernel, layer=layer),
        grid=(bsz, nblk),
        in_specs=[tok(d, False), tok(HG_KEY, False), tok(HG_VAL, False), tok(HG_VAL, False)]
        + [_resident(a.shape) for a in small],
        out_specs=tok(d, False),
        out_shape=jax.ShapeDtypeStruct((bsz, seq, d), F32),
        scratch_shapes=scratch,
        compiler_params=params,
        name=f"hgrn2_fwd_mix_pass_l{layer}",
    )(x, q, v, o_bwd, *small)


def _ffn_layer(x, p, layer, norm_pre, w_gate, w_up, w_down, norm_post, w_ple, w_ple_gate):
    bsz, seq, d = x.shape
    tokens = bsz * seq
    small = [_row(norm_pre), w_gate.astype(BF16), w_up.astype(BF16), w_down.astype(BF16), _row(norm_post),
             w_ple.astype(BF16), w_ple_gate.astype(BF16)]
    tok = lambda width: pl.BlockSpec((FFN_BLOCK, width), lambda i: (i, 0))
    out = pl.pallas_call(
        _ffn_pass_kernel,
        grid=(tokens // FFN_BLOCK,),
        in_specs=[tok(d), tok(p.shape[-1])] + [_resident(a.shape) for a in small],
        out_specs=tok(d),
        out_shape=jax.ShapeDtypeStruct((tokens, d), F32),
        compiler_params=pltpu.CompilerParams(dimension_semantics=("arbitrary",), vmem_limit_bytes=VMEM_LIMIT),
        name=f"swiglu_ple_pass_l{layer}",
    )(x.reshape(tokens, d), p.reshape(tokens, p.shape[-1]), *small)
    return out.reshape(bsz, seq, d)


def kernel(x, p, norm_mix_pre, w_in, lb_gamma_fwd, lb_gamma_bwd, hg_norm, sg_w, sg_b, sg_ln_g, sg_ln_b, w_a, w_b, w_out, norm_mix_post, norm_ffn_pre, w_gate, w_up, w_down, norm_ffn_post, w_ple, w_ple_gate):
    assert x.shape[1] % MIX_BLOCK == 0 and (x.shape[0] * x.shape[1]) % FFN_BLOCK == 0
    for l in range(w_in.shape[0]):
        x = _mixer_layer(x, l, norm_mix_pre[l], w_in[l], lb_gamma_fwd, lb_gamma_bwd, hg_norm[l], sg_w[l],
                         sg_b[l], sg_ln_g[l], sg_ln_b[l], w_a[l], w_b[l], w_out[l], norm_mix_post[l])
        x = _ffn_layer(x, p[l], l, norm_ffn_pre[l], w_gate[l], w_up[l], w_down[l], norm_ffn_post[l],
                       w_ple[l], w_ple_gate[l])
    return x
```

```python
import functools

import jax
import jax.numpy as jnp
from jax import lax
from jax.experimental import pallas as pl
from jax.experimental.pallas import tpu as pltpu

F32 = jnp.float32
BF16 = jnp.bfloat16

D_MODEL = 1024
HG_HEADS = 8
HG_KEY_DIM = 128
HG_VAL_DIM = 128
HG_KEY = HG_HEADS * HG_KEY_DIM
HG_VAL = HG_HEADS * HG_VAL_DIM
SG_GROUPS = 8
SG_GROUP_DIM = 64
SG_WIDTH = SG_GROUPS * SG_GROUP_DIM
SG_CHUNK = 128
EPS = 1e-6
LOG2E = 1.4426950408889634

MXU_WIDTH = 256
LANES = 128
SUBLANES = 8
BF16_ROWS = 16
PITCH_ALIGN_TILES = 8
CHUNK = 128
N_LEVELS = CHUNK.bit_length() - 1
SMALL_LEVELS = tuple(lg for lg in range(1, N_LEVELS) if (1 << lg) < SUBLANES)
HEADS_PER_PAIR = MXU_WIDTH // HG_KEY_DIM
N_PAIRS = HG_HEADS // HEADS_PER_PAIR

MIX_BLOCK = 512
PROJ_ROWS = 256
FFN_BLOCK = 512
VMEM_LIMIT = 52 * 1024 * 1024


def _dot(a, b):
    return jnp.dot(a, b, preferred_element_type=F32)


def _dot_tn(a, b):
    return lax.dot_general(a, b, (((0,), (0,)), ((), ())), preferred_element_type=F32)


def _rms_norm(x, g):
    return x * lax.rsqrt(jnp.mean(x * x, axis=-1, keepdims=True) + EPS) * g


def _sigmoid(x):
    return 1.0 / (1.0 + jnp.exp2(x * -LOG2E))


def _silu(x):
    return x * _sigmoid(x)


def _gelu(x):
    return 0.5 * x * (1.0 + lax.erf(x * (2.0 ** -0.5)))


def _lower_bound(gamma, layer):
    e = jnp.exp(gamma - jnp.max(gamma, axis=0, keepdims=True))
    sm = e / jnp.sum(e, axis=0, keepdims=True)
    lb = jnp.zeros((1, gamma.shape[1]), F32)
    for j in range(1, layer + 1):
        lb = lb + sm[j:j + 1, :]
    return lb


def _forget_gates(zf, lb):
    f_raw = lb + (1.0 - lb) * _sigmoid(zf)
    f = jnp.maximum(f_raw, jnp.finfo(F32).tiny)
    return 1.0 - f_raw, f, jnp.log2(f)


def _split2(x):
    hi = x.astype(BF16)
    return hi, (x - hi.astype(F32)).astype(BF16)


def _block_diag(x):
    w = x.shape[1] // 2
    z = jnp.zeros((x.shape[0], w), x.dtype)
    return jnp.concatenate([jnp.concatenate([x[:, :w], z], axis=1),
                            jnp.concatenate([z, x[:, w:]], axis=1)], axis=0)


def _block_diag_t(x):
    w = x.shape[1] // 2
    xt = x.T
    z = jnp.zeros((w, x.shape[0]), x.dtype)
    return jnp.concatenate([jnp.concatenate([xt[:w], z], axis=1),
                            jnp.concatenate([z, xt[w:]], axis=1)], axis=0)


def _level_weight(b2, lg, reverse):
    m = 1 << lg
    n, w = b2.shape
    b4 = b2.reshape(n // (2 * m), 2, m, w)
    if reverse:
        mid = b4[:, 1:2, 0:1, :]
        e = jnp.concatenate([b4[:, 0:1] - mid, mid - b4[:, 1:2]], axis=1)
    else:
        mid = b4[:, 0:1, m - 1:m, :]
        e = jnp.concatenate([mid - b4[:, 0:1], b4[:, 1:2] - mid], axis=1)
    return jnp.exp2(e.reshape(n, w))


def _query_starts(lg, reverse):
    m = 1 << lg
    return [base + (0 if reverse else m) for base in range(0, CHUNK, 2 * m)]


def _chunk_masks(reverse):
    t = lax.broadcasted_iota(jnp.int32, (CHUNK, MXU_WIDTH), 0)
    s = lax.broadcasted_iota(jnp.int32, (CHUNK, MXU_WIDTH), 1) & (CHUNK - 1)
    x = t ^ s
    order = (t < s) if reverse else (t > s)
    entry = [jnp.logical_and((x >> lg) == 1, order) for lg in range(N_LEVELS) if (1 << lg) < SUBLANES]
    s_tile = lax.broadcasted_iota(jnp.int32, (SUBLANES, MXU_WIDTH), 1) & (CHUNK - 1)
    key_half = {lg: s_tile >> lg for lg in range(N_LEVELS) if (1 << lg) >= SUBLANES}
    return entry, t == s, key_half, _prefix_matrices(reverse)


def _prefix_matrices(reverse):
    t = lax.broadcasted_iota(jnp.int32, (CHUNK, CHUNK), 0)
    u = lax.broadcasted_iota(jnp.int32, (CHUNK, CHUNK), 1)
    mats = [(t <= u) if reverse else (t >= u)]
    for lg in SMALL_LEVELS:
        first_half = ((t >> lg) & 1) == 0
        after_t = (u >= t) if reverse else (u > t)
        mats.append(jnp.logical_and((t >> lg) == (u >> lg), first_half == after_t))
    return jnp.concatenate([jnp.concatenate([m, m], axis=1) for m in mats], axis=0).astype(BF16)


def _level_operands(qp, kp, w, lg, reverse):
    m = 1 << lg
    if m < BF16_ROWS:
        return qp * w, kp * w
    pieces = []
    for s in _query_starts(lg, reverse):
        other = slice(s + m, s + 2 * m) if reverse else slice(s - m, s)
        pieces += [qp[s:s + m], kp[other]] if reverse else [kp[other], qp[s:s + m]]
    xw = jnp.concatenate(pieces, axis=0) * w
    return xw, xw


def _merge_level(acc, xq, xk, lg, reverse, entry, key_half):
    m = 1 << lg
    if m < SUBLANES:
        return jnp.where(entry[lg], _dot(xq, _block_diag_t(xk)), acc)
    starts = _query_starts(lg, reverse)
    if m >= BF16_ROWS:
        s_q = _dot(jnp.concatenate([xq[s:s + m] for s in starts], axis=0), _block_diag_t(xk))
        blocks = [s_q[i * m:(i + 1) * m] for i in range(len(starts))]
    else:
        s_all = _dot(xq, _block_diag_t(xk))
        blocks = [s_all[s:s + m] for s in starts]
    pieces = []
    for s, blk in zip(starts, blocks):
        keys = (s >> lg) + (1 if reverse else -1)
        tiles = (m // SUBLANES, SUBLANES, MXU_WIDTH)
        merged = jnp.where((key_half[lg] == keys)[None], blk.reshape(tiles),
                           acc[s:s + m].reshape(tiles)).reshape(m, MXU_WIDTH)
        other = acc[s + m:s + 2 * m] if reverse else acc[s - m:s]
        pieces += [merged, other] if reverse else [other, merged]
    return jnp.concatenate(pieces, axis=0)


def _hgrn2_chunk(q, k, f, log2f, v, state_ref, reverse, masks):
    entry, diag_mask, key_half, prefix = masks
    sums = _dot(prefix, jnp.concatenate(_split2(log2f), axis=0))
    b2_all = sums[:CHUNK]
    small_exponents = {lg: sums[(i + 1) * CHUNK:(i + 2) * CHUNK] for i, lg in enumerate(SMALL_LEVELS)}
    qf = q * f.astype(BF16)
    edge = 0 if reverse else CHUNK - 1

    outs = []
    for p in range(N_PAIRS):
        sl = slice(p * MXU_WIDTH, (p + 1) * MXU_WIDTH)
        qp, kp, b2, vp = q[:, sl], k[:, sl], b2_all[:, sl], v[:, sl]

        qk = (qp * kp).astype(F32)
        d_pair = [jnp.broadcast_to(jnp.sum(qk[:, j * HG_KEY_DIM:(j + 1) * HG_KEY_DIM], axis=1, keepdims=True),
                                   (CHUNK, CHUNK)) for j in range(HEADS_PER_PAIR)]
        acc = jnp.where(diag_mask, jnp.concatenate(d_pair, axis=1), 0.0)
        for lg in range(N_LEVELS):
            if lg == 0:
                xq, xk = qf[:, sl], kp
            else:
                w = jnp.exp2(small_exponents[lg][:, sl]) if lg in SMALL_LEVELS else _level_weight(b2, lg, reverse)
                xq, xk = _level_operands(qp, kp, w.astype(BF16), lg, reverse)
            acc = _merge_level(acc, xq, xk, lg, reverse, entry, key_half)
        o_intra = _dot(acc.astype(BF16), _block_diag(vp))

        h0 = p * HEADS_PER_PAIR
        st = [state_ref[h0 + j] for j in range(HEADS_PER_PAIR)]
        z = jnp.zeros((HG_VAL_DIM, HG_KEY_DIM), BF16)
        st_bd = jnp.concatenate([jnp.concatenate([st[0].astype(BF16).T, z], axis=1),
                                 jnp.concatenate([z, st[1].astype(BF16).T], axis=1)], axis=0)
        o_inter = _dot(qp * jnp.exp2(b2).astype(BF16), st_bd)
        outs.append(o_intra + o_inter)

        b2_edge = b2[edge:edge + 1, :]
        k_to_edge = kp * jnp.exp2(b2_edge - b2).astype(BF16)
        decay = jnp.exp2(b2_edge)
        for j in range(HEADS_PER_PAIR):
            hs = slice(j * HG_KEY_DIM, (j + 1) * HG_KEY_DIM)
            state_ref[h0 + j] = st[j] * decay[:, hs] + _dot_tn(vp[:, hs], k_to_edge[:, hs])
    return jnp.concatenate(outs, axis=1)


def _bwd_pass_kernel(x_ref, g_pre_ref, w_ref, gamma_ref, q_ref, v_ref, o_ref, state_ref, *, layer):
    @pl.when(pl.program_id(1) == 0)
    def _():
        state_ref[...] = jnp.zeros_like(state_ref)

    masks = _chunk_masks(True)
    lb = _lower_bound(gamma_ref[...], layer)
    for sb in reversed(range(MIX_BLOCK // PROJ_ROWS)):
        base = sb * PROJ_ROWS
        h = _rms_norm(x_ref[0, base:base + PROJ_ROWS, :], g_pre_ref[...]).astype(BF16)
        q = _silu(_dot(h, w_ref[:, 0:HG_KEY])).astype(BF16)
        q_ref[0, base:base + PROJ_ROWS, :] = q
        v = _dot(h, w_ref[:, 2 * HG_KEY:2 * HG_KEY + HG_VAL]).astype(BF16)
        v_ref[0, base:base + PROJ_ROWS, :] = v
        k, f, log2f = _forget_gates(_dot(h, w_ref[:, HG_KEY:2 * HG_KEY]), lb)
        k = k.astype(BF16)
        for c in reversed(range(PROJ_ROWS // CHUNK)):
            rows = slice(c * CHUNK, (c + 1) * CHUNK)
            o_ref[0, base + c * CHUNK:base + (c + 1) * CHUNK, :] = _hgrn2_chunk(
                q[rows], k[rows], f[rows], log2f[rows], v[rows], state_ref, True, masks)


def _fwd_pass_kernel(x_ref, q_ref, v_ref, ob_ref, g_pre_ref, w_ref, gamma_ref, hg_norm_ref, sg_w_ref,
                     sg_bias_ref, ln_g_ref, ln_b_ref, w_a_ref, w_b_ref, w_out_ref, g_post_ref,
                     out_ref, state_ref, *, layer):
    @pl.when(pl.program_id(1) == 0)
    def _():
        state_ref[...] = jnp.zeros_like(state_ref)

    c_f, c_g, c_u, c_v, c_ga, c_gb = 0, HG_KEY, HG_KEY + HG_VAL, HG_KEY + HG_VAL + SG_WIDTH, \
        HG_KEY + HG_VAL + 2 * SG_WIDTH, HG_KEY + HG_VAL + 2 * SG_WIDTH + D_MODEL
    masks = _chunk_masks(False)
    lb = _lower_bound(gamma_ref[...], layer)

    for sb in range(MIX_BLOCK // PROJ_ROWS):
        blk = slice(sb * PROJ_ROWS, (sb + 1) * PROJ_ROWS)
        x = x_ref[0, blk, :]
        h = _rms_norm(x, g_pre_ref[...]).astype(BF16)

        k, f, log2f = _forget_gates(_dot(h, w_ref[:, c_f:c_g]), lb)
        k = k.astype(BF16)
        q = q_ref[0, blk, :]
        v = v_ref[0, blk, :]
        o_chunks = []
        for c in range(PROJ_ROWS // CHUNK):
            rows = slice(c * CHUNK, (c + 1) * CHUNK)
            o_chunks.append(_hgrn2_chunk(q[rows], k[rows], f[rows], log2f[rows], v[rows], state_ref, False, masks))
        o = jnp.concatenate(o_chunks, axis=0) + ob_ref[0, blk, :]
        heads = []
        for hd in range(HG_HEADS):
            oh = o[:, hd * HG_VAL_DIM:(hd + 1) * HG_VAL_DIM]
            heads.append(oh * lax.rsqrt(jnp.mean(oh * oh, axis=-1, keepdims=True) + EPS))
        a_out = jnp.concatenate(heads, axis=1) * hg_norm_ref[...] * _silu(_dot(h, w_ref[:, c_g:c_u]))
        a_proj = _dot(a_out.astype(BF16), w_a_ref[:, :D_MODEL])

        u = _gelu(_dot(h, w_ref[:, c_u:c_v]))
        gv = _gelu(_dot(h, w_ref[:, c_v:c_ga]))
        mu = jnp.mean(gv, axis=-1, keepdims=True)
        gc = gv - mu
        vn = (gc * lax.rsqrt(jnp.mean(gc * gc, axis=-1, keepdims=True) + EPS) * ln_g_ref[...]
              + ln_b_ref[...]).astype(BF16)
        sg_rows = []
        for c in range(PROJ_ROWS // SG_CHUNK):
            rows = slice(c * SG_CHUNK, (c + 1) * SG_CHUNK)
            cols = [_dot(sg_w_ref[gp], _block_diag(vn[rows, gp * LANES:(gp + 1) * LANES]))
                    for gp in range(SG_WIDTH // LANES)]
            sg_rows.append(jnp.concatenate(cols, axis=1) + sg_bias_ref[...])
        b_out = u * jnp.concatenate(sg_rows, axis=0)
        b_proj = _dot(b_out.astype(BF16), w_b_ref[:, :D_MODEL])

        merged = (_sigmoid(_dot(h, w_ref[:, c_ga:c_gb])) * a_proj
                  + _sigmoid(_dot(h, w_ref[:, c_gb:c_gb + D_MODEL])) * b_proj)
        mix = _dot(merged.astype(BF16), w_out_ref[:, :D_MODEL])
        out_ref[0, blk, :] = x + _rms_norm(mix, g_post_ref[...])


def _ffn_pass_kernel(x_ref, p_ref, g_pre_ref, w_gate_ref, w_up_ref, w_down_ref, g_post_ref,
                     w_ple_ref, w_ple_gate_ref, out_ref):
    x = x_ref[...]
    h = _rms_norm(x, g_pre_ref[...]).astype(BF16)
    act = _silu(_dot(h, w_gate_ref[...])) * _dot(h, w_up_ref[...])
    ff = _dot(act.astype(BF16), w_down_ref[:, :D_MODEL])
    x = x + _rms_norm(ff, g_post_ref[...])
    emb = _dot(p_ref[...].astype(BF16), w_ple_ref[:, :D_MODEL])
    out_ref[...] = x + emb * _sigmoid(_dot(x.astype(BF16), w_ple_gate_ref[:, :D_MODEL]))


def _resident(shape):
    return pl.BlockSpec(shape, lambda *_: (0,) * len(shape), pipeline_mode=pl.Buffered(1))


def _row(a):
    return a.reshape(1, -1).astype(F32)


def _weight(w):
    w = w.astype(BF16)
    if (w.shape[1] // LANES) % PITCH_ALIGN_TILES == 0:
        w = jnp.pad(w, ((0, 0), (0, LANES)))
    return w


def _mixer_layer(x, layer, norm_pre, w_in, gamma_f, gamma_b, hg_norm, sg_w, sg_b, ln_g, ln_b, w_a, w_b,
                 w_out, norm_post):
    bsz, seq, d = x.shape
    nblk = seq // MIX_BLOCK
    scratch = [pltpu.VMEM((HG_HEADS, HG_VAL_DIM, HG_KEY_DIM), F32)]
    params = pltpu.CompilerParams(dimension_semantics=("arbitrary", "arbitrary"), vmem_limit_bytes=VMEM_LIMIT)

    o_ff, o_fb, o_i, o_g = HG_KEY, 2 * HG_KEY, 3 * HG_KEY, 3 * HG_KEY + HG_VAL
    w_bwd = _weight(jnp.concatenate([w_in[:, :o_ff], w_in[:, o_fb:o_i], w_in[:, o_i:o_g]], axis=1))
    w_fwd = _weight(jnp.concatenate([w_in[:, o_ff:o_fb], w_in[:, o_g:]], axis=1))

    tok = lambda width, rev: pl.BlockSpec(
        (1, MIX_BLOCK, width), (lambda b, i: (b, nblk - 1 - i, 0)) if rev else (lambda b, i: (b, i, 0)))

    q, v, o_bwd = pl.pallas_call(
        functools.partial(_bwd_pass_kernel, layer=layer),
        grid=(bsz, nblk),
        in_specs=[tok(d, True), _resident((1, d)), _resident(w_bwd.shape), _resident(gamma_b.shape)],
        out_specs=[tok(HG_KEY, True), tok(HG_VAL, True), tok(HG_VAL, True)],
        out_shape=[jax.ShapeDtypeStruct((bsz, seq, HG_KEY), BF16),
                   jax.ShapeDtypeStruct((bsz, seq, HG_VAL), BF16),
                   jax.ShapeDtypeStruct((bsz, seq, HG_VAL), F32)],
        scratch_shapes=scratch,
        compiler_params=params,
        name=f"hgrn2_bwd_pass_l{layer}",
    )(x, _row(norm_pre), w_bwd, gamma_b)

    sg_w_pairs = sg_w.reshape(SG_GROUPS // 2, 2, SG_CHUNK, SG_CHUNK).transpose(0, 2, 1, 3).reshape(
        SG_GROUPS // 2, SG_CHUNK, 2 * SG_CHUNK).astype(BF16)
    sg_bias = jnp.repeat(sg_b.T, SG_GROUP_DIM, axis=1).astype(F32)
    small = [_row(norm_pre), w_fwd, gamma_f, _row(hg_norm), sg_w_pairs, sg_bias, _row(ln_g), _row(ln_b),
             _weight(w_a), _weight(w_b), _weight(w_out), _row(norm_post)]
    return pl.pallas_call(
        functools.partial(_fwd_pass_kernel, layer=layer),
        grid=(bsz, nblk),
        in_specs=[tok(d, False), tok(HG_KEY, False), tok(HG_VAL, False), tok(HG_VAL, False)]
        + [_resident(a.shape) for a in small],
        out_specs=tok(d, False),
        out_shape=jax.ShapeDtypeStruct((bsz, seq, d), F32),
        scratch_shapes=scratch,
        compiler_params=params,
        name=f"hgrn2_fwd_mix_pass_l{layer}",
    )(x, q, v, o_bwd, *small)


def _ffn_layer(x, p, layer, norm_pre, w_gate, w_up, w_down, norm_post, w_ple, w_ple_gate):
    bsz, seq, d = x.shape
    tokens = bsz * seq
    small = [_row(norm_pre), _weight(w_gate), _weight(w_up), _weight(w_down), _row(norm_post),
             _weight(w_ple), _weight(w_ple_gate)]
    tok = lambda width: pl.BlockSpec((FFN_BLOCK, width), lambda i: (i, 0))
    out = pl.pallas_call(
        _ffn_pass_kernel,
        grid=(tokens // FFN_BLOCK,),
        in_specs=[tok(d), tok(p.shape[-1])] + [_resident(a.shape) for a in small],
        out_specs=tok(d),
        out_shape=jax.ShapeDtypeStruct((tokens, d), F32),
        compiler_params=pltpu.CompilerParams(dimension_semantics=("arbitrary",), vmem_limit_bytes=VMEM_LIMIT),
        name=f"swiglu_ple_pass_l{layer}",
    )(x.reshape(tokens, d), p.reshape(tokens, p.shape[-1]), *small)
    return out.reshape(bsz, seq, d)


def kernel(x, p, norm_mix_pre, w_in, lb_gamma_fwd, lb_gamma_bwd, hg_norm, sg_w, sg_b, sg_ln_g, sg_ln_b, w_a, w_b, w_out, norm_mix_post, norm_ffn_pre, w_gate, w_up, w_down, norm_ffn_post, w_ple, w_ple_gate):
    assert x.shape[1] % MIX_BLOCK == 0 and (x.shape[0] * x.shape[1]) % FFN_BLOCK == 0
    for l in range(w_in.shape[0]):
        x = _mixer_layer(x, l, norm_mix_pre[l], w_in[l], lb_gamma_fwd, lb_gamma_bwd, hg_norm[l], sg_w[l],
                         sg_b[l], sg_ln_g[l], sg_ln_b[l], w_a[l], w_b[l], w_out[l], norm_mix_post[l])
        x = _ffn_layer(x, p[l], l, norm_ffn_pre[l], w_gate[l], w_up[l], w_down[l], norm_ffn_post[l],
                       w_ple[l], w_ple_gate[l])
    return x
```

```python
import functools

import jax
import jax.numpy as jnp
from jax import lax
from jax.experimental import pallas as pl
from jax.experimental.pallas import tpu as pltpu

F32 = jnp.float32
BF16 = jnp.bfloat16

D_MODEL = 1024
HG_HEADS = 8
HG_KEY_DIM = 128
HG_VAL_DIM = 128
HG_KEY = HG_HEADS * HG_KEY_DIM
HG_VAL = HG_HEADS * HG_VAL_DIM
SG_GROUPS = 8
SG_GROUP_DIM = 64
SG_WIDTH = SG_GROUPS * SG_GROUP_DIM
SG_CHUNK = 128
EPS = 1e-6
LOG2E = 1.4426950408889634

MXU_WIDTH = 256
LANES = 128
SUBLANES = 8
BF16_ROWS = 16
PITCH_ALIGN_TILES = 8
CHUNK = 128
N_LEVELS = CHUNK.bit_length() - 1
SMALL_LEVELS = tuple(lg for lg in range(1, N_LEVELS) if (1 << lg) < SUBLANES)
HEADS_PER_PAIR = MXU_WIDTH // HG_KEY_DIM
N_PAIRS = HG_HEADS // HEADS_PER_PAIR

IN_NAMES = ("q", "f_fwd", "f_bwd", "i", "g", "u", "v", "gate_a", "gate_b")
IN_WIDTHS = (HG_KEY, HG_KEY, HG_KEY, HG_VAL, HG_VAL, SG_WIDTH, SG_WIDTH, D_MODEL, D_MODEL)
IN_COLS = {n: slice(sum(IN_WIDTHS[:i]), sum(IN_WIDTHS[:i + 1])) for i, n in enumerate(IN_NAMES)}

MIX_BLOCK = 512
PROJ_ROWS = 256
FFN_BLOCK = 512
VMEM_LIMIT = 52 * 1024 * 1024


def _dot(a, b):
    return jnp.dot(a, b, preferred_element_type=F32)


def _dot_tn(a, b):
    return lax.dot_general(a, b, (((0,), (0,)), ((), ())), preferred_element_type=F32)


def _rms_norm(x, g):
    return x * lax.rsqrt(jnp.mean(x * x, axis=-1, keepdims=True) + EPS) * g


def _sigmoid(x):
    return 1.0 / (1.0 + jnp.exp2(x * -LOG2E))


def _silu(x):
    return x * _sigmoid(x)


def _gelu(x):
    return 0.5 * x * (1.0 + lax.erf(x * (2.0 ** -0.5)))


def _lower_bound(gamma, layer):
    e = jnp.exp(gamma - jnp.max(gamma, axis=0, keepdims=True))
    sm = e / jnp.sum(e, axis=0, keepdims=True)
    lb = jnp.zeros((1, gamma.shape[1]), F32)
    for j in range(1, layer + 1):
        lb = lb + sm[j:j + 1, :]
    return lb


def _forget_gates(zf, lb):
    f_raw = lb + (1.0 - lb) * _sigmoid(zf)
    f = jnp.maximum(f_raw, jnp.finfo(F32).tiny)
    return 1.0 - f_raw, f, jnp.log2(f)


def _split2(x):
    hi = x.astype(BF16)
    return hi, (x - hi.astype(F32)).astype(BF16)


def _block_diag(x):
    w = x.shape[1] // 2
    z = jnp.zeros((x.shape[0], w), x.dtype)
    return jnp.concatenate([jnp.concatenate([x[:, :w], z], axis=1),
                            jnp.concatenate([z, x[:, w:]], axis=1)], axis=0)


def _block_diag_t(x):
    w = x.shape[1] // 2
    xt = x.T
    z = jnp.zeros((w, x.shape[0]), x.dtype)
    return jnp.concatenate([jnp.concatenate([xt[:w], z], axis=1),
                            jnp.concatenate([z, xt[w:]], axis=1)], axis=0)


def _level_weight(b2, lg, reverse):
    m = 1 << lg
    n, w = b2.shape
    b4 = b2.reshape(n // (2 * m), 2, m, w)
    if reverse:
        mid = b4[:, 1:2, 0:1, :]
        e = jnp.concatenate([b4[:, 0:1] - mid, mid - b4[:, 1:2]], axis=1)
    else:
        mid = b4[:, 0:1, m - 1:m, :]
        e = jnp.concatenate([mid - b4[:, 0:1], b4[:, 1:2] - mid], axis=1)
    return jnp.exp2(e.reshape(n, w))


def _query_starts(lg, reverse):
    m = 1 << lg
    return [base + (0 if reverse else m) for base in range(0, CHUNK, 2 * m)]


def _chunk_masks(reverse):
    t = lax.broadcasted_iota(jnp.int32, (CHUNK, MXU_WIDTH), 0)
    s = lax.broadcasted_iota(jnp.int32, (CHUNK, MXU_WIDTH), 1) & (CHUNK - 1)
    x = t ^ s
    order = (t < s) if reverse else (t > s)
    entry = [jnp.logical_and((x >> lg) == 1, order) for lg in range(N_LEVELS) if (1 << lg) < SUBLANES]
    s_tile = lax.broadcasted_iota(jnp.int32, (SUBLANES, MXU_WIDTH), 1) & (CHUNK - 1)
    key_half = {lg: s_tile >> lg for lg in range(N_LEVELS) if (1 << lg) >= SUBLANES}
    return entry, t == s, key_half, _prefix_matrices(reverse)


def _prefix_matrices(reverse):
    t = lax.broadcasted_iota(jnp.int32, (CHUNK, CHUNK), 0)
    u = lax.broadcasted_iota(jnp.int32, (CHUNK, CHUNK), 1)
    mats = [(t <= u) if reverse else (t >= u)]
    for lg in SMALL_LEVELS:
        first_half = ((t >> lg) & 1) == 0
        after_t = (u >= t) if reverse else (u > t)
        mats.append(jnp.logical_and((t >> lg) == (u >> lg), first_half == after_t))
    return jnp.concatenate([jnp.concatenate([m, m], axis=1) for m in mats], axis=0).astype(BF16)


def _level_operands(qp, kp, w, lg, reverse):
    m = 1 << lg
    if m < BF16_ROWS:
        return qp * w, kp * w
    pieces = []
    for s in _query_starts(lg, reverse):
        other = slice(s + m, s + 2 * m) if reverse else slice(s - m, s)
        pieces += [qp[s:s + m], kp[other]] if reverse else [kp[other], qp[s:s + m]]
    xw = jnp.concatenate(pieces, axis=0) * w
    return xw, xw


def _merge_level(acc, xq, xk, lg, reverse, entry, key_half):
    m = 1 << lg
    if m < SUBLANES:
        return jnp.where(entry[lg], _dot(xq, _block_diag_t(xk)), acc)
    starts = _query_starts(lg, reverse)
    if m >= BF16_ROWS:
        s_q = _dot(jnp.concatenate([xq[s:s + m] for s in starts], axis=0), _block_diag_t(xk))
        blocks = [s_q[i * m:(i + 1) * m] for i in range(len(starts))]
    else:
        s_all = _dot(xq, _block_diag_t(xk))
        blocks = [s_all[s:s + m] for s in starts]
    pieces = []
    for s, blk in zip(starts, blocks):
        keys = (s >> lg) + (1 if reverse else -1)
        tiles = (m // SUBLANES, SUBLANES, MXU_WIDTH)
        merged = jnp.where((key_half[lg] == keys)[None], blk.reshape(tiles),
                           acc[s:s + m].reshape(tiles)).reshape(m, MXU_WIDTH)
        other = acc[s + m:s + 2 * m] if reverse else acc[s - m:s]
        pieces += [merged, other] if reverse else [other, merged]
    return jnp.concatenate(pieces, axis=0)


def _hgrn2_chunk(q, k, f, log2f, v, state_ref, reverse, masks):
    entry, diag_mask, key_half, prefix = masks
    sums = _dot(prefix, jnp.concatenate(_split2(log2f), axis=0))
    b2_all = sums[:CHUNK]
    small_exponents = {lg: sums[(i + 1) * CHUNK:(i + 2) * CHUNK] for i, lg in enumerate(SMALL_LEVELS)}
    qf = q * f.astype(BF16)
    edge = 0 if reverse else CHUNK - 1

    outs = []
    for p in range(N_PAIRS):
        sl = slice(p * MXU_WIDTH, (p + 1) * MXU_WIDTH)
        qp, kp, b2, vp = q[:, sl], k[:, sl], b2_all[:, sl], v[:, sl]

        qk = (qp * kp).astype(F32)
        d_pair = [jnp.broadcast_to(jnp.sum(qk[:, j * HG_KEY_DIM:(j + 1) * HG_KEY_DIM], axis=1, keepdims=True),
                                   (CHUNK, CHUNK)) for j in range(HEADS_PER_PAIR)]
        acc = jnp.where(diag_mask, jnp.concatenate(d_pair, axis=1), 0.0)
        for lg in range(N_LEVELS):
            if lg == 0:
                xq, xk = qf[:, sl], kp
            else:
                w = jnp.exp2(small_exponents[lg][:, sl]) if lg in SMALL_LEVELS else _level_weight(b2, lg, reverse)
                xq, xk = _level_operands(qp, kp, w.astype(BF16), lg, reverse)
            acc = _merge_level(acc, xq, xk, lg, reverse, entry, key_half)
        o_intra = _dot(acc.astype(BF16), _block_diag(vp))

        h0 = p * HEADS_PER_PAIR
        st = [state_ref[h0 + j] for j in range(HEADS_PER_PAIR)]
        z = jnp.zeros((HG_VAL_DIM, HG_KEY_DIM), BF16)
        st_bd = jnp.concatenate([jnp.concatenate([st[0].astype(BF16).T, z], axis=1),
                                 jnp.concatenate([z, st[1].astype(BF16).T], axis=1)], axis=0)
        o_inter = _dot(qp * jnp.exp2(b2).astype(BF16), st_bd)
        outs.append(o_intra + o_inter)

        b2_edge = b2[edge:edge + 1, :]
        k_to_edge = kp * jnp.exp2(b2_edge - b2).astype(BF16)
        decay = jnp.exp2(b2_edge)
        for j in range(HEADS_PER_PAIR):
            hs = slice(j * HG_KEY_DIM, (j + 1) * HG_KEY_DIM)
            state_ref[h0 + j] = st[j] * decay[:, hs] + _dot_tn(vp[:, hs], k_to_edge[:, hs])
    return jnp.concatenate(outs, axis=1)


def _bwd_pass_kernel(x_ref, g_pre_ref, w_ref, gamma_ref, q_ref, v_ref, o_ref, state_ref, *, layer):
    @pl.when(pl.program_id(1) == 0)
    def _():
        state_ref[...] = jnp.zeros_like(state_ref)

    masks = _chunk_masks(True)
    lb = _lower_bound(gamma_ref[...], layer)
    for sb in reversed(range(MIX_BLOCK // PROJ_ROWS)):
        base = sb * PROJ_ROWS
        h = _rms_norm(x_ref[0, base:base + PROJ_ROWS, :], g_pre_ref[...]).astype(BF16)
        q = _silu(_dot(h, w_ref[:, IN_COLS["q"]])).astype(BF16)
        q_ref[0, base:base + PROJ_ROWS, :] = q
        v = _dot(h, w_ref[:, IN_COLS["i"]]).astype(BF16)
        v_ref[0, base:base + PROJ_ROWS, :] = v
        k, f, log2f = _forget_gates(_dot(h, w_ref[:, IN_COLS["f_bwd"]]), lb)
        k = k.astype(BF16)
        for c in reversed(range(PROJ_ROWS // CHUNK)):
            rows = slice(c * CHUNK, (c + 1) * CHUNK)
            o_ref[0, base + c * CHUNK:base + (c + 1) * CHUNK, :] = _hgrn2_chunk(
                q[rows], k[rows], f[rows], log2f[rows], v[rows], state_ref, True, masks)


def _fwd_pass_kernel(x_ref, q_ref, v_ref, ob_ref, g_pre_ref, w_ref, gamma_ref, hg_norm_ref, sg_w_ref,
                     sg_bias_ref, ln_g_ref, ln_b_ref, w_a_ref, w_b_ref, w_out_ref, g_post_ref,
                     out_ref, state_ref, *, layer):
    @pl.when(pl.program_id(1) == 0)
    def _():
        state_ref[...] = jnp.zeros_like(state_ref)

    masks = _chunk_masks(False)
    lb = _lower_bound(gamma_ref[...], layer)

    for sb in range(MIX_BLOCK // PROJ_ROWS):
        blk = slice(sb * PROJ_ROWS, (sb + 1) * PROJ_ROWS)
        x = x_ref[0, blk, :]
        h = _rms_norm(x, g_pre_ref[...]).astype(BF16)

        k, f, log2f = _forget_gates(_dot(h, w_ref[:, IN_COLS["f_fwd"]]), lb)
        k = k.astype(BF16)
        q = q_ref[0, blk, :]
        v = v_ref[0, blk, :]
        o_chunks = []
        for c in range(PROJ_ROWS // CHUNK):
            rows = slice(c * CHUNK, (c + 1) * CHUNK)
            o_chunks.append(_hgrn2_chunk(q[rows], k[rows], f[rows], log2f[rows], v[rows], state_ref, False, masks))
        o = jnp.concatenate(o_chunks, axis=0) + ob_ref[0, blk, :]
        heads = []
        for hd in range(HG_HEADS):
            oh = o[:, hd * HG_VAL_DIM:(hd + 1) * HG_VAL_DIM]
            heads.append(oh * lax.rsqrt(jnp.mean(oh * oh, axis=-1, keepdims=True) + EPS))
        a_out = jnp.concatenate(heads, axis=1) * hg_norm_ref[...] * _silu(_dot(h, w_ref[:, IN_COLS["g"]]))
        a_proj = _dot(a_out.astype(BF16), w_a_ref[:, :D_MODEL])

        u = _gelu(_dot(h, w_ref[:, IN_COLS["u"]]))
        gv = _gelu(_dot(h, w_ref[:, IN_COLS["v"]]))
        mu = jnp.mean(gv, axis=-1, keepdims=True)
        gc = gv - mu
        vn = (gc * lax.rsqrt(jnp.mean(gc * gc, axis=-1, keepdims=True) + EPS) * ln_g_ref[...]
              + ln_b_ref[...]).astype(BF16)
        sg_rows = []
        for c in range(PROJ_ROWS // SG_CHUNK):
            rows = slice(c * SG_CHUNK, (c + 1) * SG_CHUNK)
            cols = [_dot(sg_w_ref[gp], _block_diag(vn[rows, gp * LANES:(gp + 1) * LANES]))
                    for gp in range(SG_WIDTH // LANES)]
            sg_rows.append(jnp.concatenate(cols, axis=1) + sg_bias_ref[...])
        b_out = u * jnp.concatenate(sg_rows, axis=0)
        b_proj = _dot(b_out.astype(BF16), w_b_ref[:, :D_MODEL])

        merged = (_sigmoid(_dot(h, w_ref[:, IN_COLS["gate_a"]])) * a_proj
                  + _sigmoid(_dot(h, w_ref[:, IN_COLS["gate_b"]])) * b_proj)
        mix = _dot(merged.astype(BF16), w_out_ref[:, :D_MODEL])
        out_ref[0, blk, :] = x + _rms_norm(mix, g_post_ref[...])


def _ffn_pass_kernel(x_ref, p_ref, g_pre_ref, w_gate_ref, w_up_ref, w_down_ref, g_post_ref,
                     w_ple_ref, w_ple_gate_ref, out_ref):
    x = x_ref[...]
    h = _rms_norm(x, g_pre_ref[...]).astype(BF16)
    act = _silu(_dot(h, w_gate_ref[...])) * _dot(h, w_up_ref[...])
    ff = _dot(act.astype(BF16), w_down_ref[:, :D_MODEL])
    x = x + _rms_norm(ff, g_post_ref[...])
    emb = _dot(p_ref[...].astype(BF16), w_ple_ref[:, :D_MODEL])
    out_ref[...] = x + emb * _sigmoid(_dot(x.astype(BF16), w_ple_gate_ref[:, :D_MODEL]))


def _resident(shape):
    return pl.BlockSpec(shape, lambda *_: (0,) * len(shape), pipeline_mode=pl.Buffered(1))


def _layer_of(a, layer):
    rest = a.shape[1:]
    return pl.BlockSpec((None,) + rest, lambda *_: (layer,) + (0,) * len(rest), pipeline_mode=pl.Buffered(1))


def _rows(a):
    return a.reshape(a.shape[0], 1, -1).astype(F32)


def _weight(w):
    w = w.astype(BF16)
    if (w.shape[-1] // LANES) % PITCH_ALIGN_TILES == 0:
        w = jnp.pad(w, ((0, 0),) * (w.ndim - 1) + ((0, LANES),))
    return w


def _mixer_layer(x, layer, norm_pre, w_in, gamma_f, gamma_b, hg_norm, sg_w_pairs, sg_bias, ln_g, ln_b,
                 w_a, w_b, w_out, norm_post):
    bsz, seq, d = x.shape
    nblk = seq // MIX_BLOCK
    scratch = [pltpu.VMEM((HG_HEADS, HG_VAL_DIM, HG_KEY_DIM), F32)]
    params = pltpu.CompilerParams(dimension_semantics=("arbitrary", "arbitrary"), vmem_limit_bytes=VMEM_LIMIT)
    tok = lambda width, rev: pl.BlockSpec(
        (1, MIX_BLOCK, width), (lambda b, i: (b, nblk - 1 - i, 0)) if rev else (lambda b, i: (b, i, 0)))

    q, v, o_bwd = pl.pallas_call(
        functools.partial(_bwd_pass_kernel, layer=layer),
        grid=(bsz, nblk),
        in_specs=[tok(d, True), _layer_of(norm_pre, layer), _layer_of(w_in, layer), _resident(gamma_b.shape)],
        out_specs=[tok(HG_KEY, True), tok(HG_VAL, True), tok(HG_VAL, True)],
        out_shape=[jax.ShapeDtypeStruct((bsz, seq, HG_KEY), BF16),
                   jax.ShapeDtypeStruct((bsz, seq, HG_VAL), BF16),
                   jax.ShapeDtypeStruct((bsz, seq, HG_VAL), F32)],
        scratch_shapes=scratch,
        compiler_params=params,
        name=f"hgrn2_bwd_pass_l{layer}",
    )(x, norm_pre, w_in, gamma_b)

    stacked = [norm_pre, w_in, None, hg_norm, sg_w_pairs, sg_bias, ln_g, ln_b, w_a, w_b, w_out, norm_post]
    specs = [_resident(gamma_f.shape) if a is None else _layer_of(a, layer) for a in stacked]
    stacked[2] = gamma_f
    return pl.pallas_call(
        functools.partial(_fwd_pass_kernel, layer=layer),
        grid=(bsz, nblk),
        in_specs=[tok(d, False), tok(HG_KEY, False), tok(HG_VAL, False), tok(HG_VAL, False)] + specs,
        out_specs=tok(d, False),
        out_shape=jax.ShapeDtypeStruct((bsz, seq, d), F32),
        scratch_shapes=scratch,
        compiler_params=params,
        name=f"hgrn2_fwd_mix_pass_l{layer}",
    )(x, q, v, o_bwd, *stacked)


def _ffn_layer(x, p, layer, norm_pre, w_gate, w_up, w_down, norm_post, w_ple, w_ple_gate):
    bsz, seq, d = x.shape
    tokens = bsz * seq
    stacked = [norm_pre, w_gate, w_up, w_down, norm_post, w_ple, w_ple_gate]
    tok = lambda width: pl.BlockSpec((FFN_BLOCK, width), lambda i: (i, 0))
    out = pl.pallas_call(
        _ffn_pass_kernel,
        grid=(tokens // FFN_BLOCK,),
        in_specs=[tok(d), pl.BlockSpec((None, FFN_BLOCK, p.shape[-1]), lambda i: (layer, i, 0))]
        + [_layer_of(a, layer) for a in stacked],
        out_specs=tok(d),
        out_shape=jax.ShapeDtypeStruct((tokens, d), F32),
        compiler_params=pltpu.CompilerParams(dimension_semantics=("arbitrary",), vmem_limit_bytes=VMEM_LIMIT),
        name=f"swiglu_ple_pass_l{layer}",
    )(x.reshape(tokens, d), p, *stacked)
    return out.reshape(bsz, seq, d)


def kernel(x, p, norm_mix_pre, w_in, lb_gamma_fwd, lb_gamma_bwd, hg_norm, sg_w, sg_b, sg_ln_g, sg_ln_b, w_a, w_b, w_out, norm_mix_post, norm_ffn_pre, w_gate, w_up, w_down, norm_ffn_post, w_ple, w_ple_gate):
    assert x.shape[1] % MIX_BLOCK == 0 and (x.shape[0] * x.shape[1]) % FFN_BLOCK == 0
    depth = w_in.shape[0]
    sg_w_pairs = sg_w.reshape(depth, SG_GROUPS // 2, 2, SG_CHUNK, SG_CHUNK).transpose(0, 1, 3, 2, 4).reshape(
        depth, SG_GROUPS // 2, SG_CHUNK, 2 * SG_CHUNK).astype(BF16)
    sg_bias = jnp.repeat(sg_b.transpose(0, 2, 1), SG_GROUP_DIM, axis=2).astype(F32)
    mixer = [_rows(norm_mix_pre), _weight(w_in), lb_gamma_fwd, lb_gamma_bwd, _rows(hg_norm), sg_w_pairs, sg_bias,
             _rows(sg_ln_g), _rows(sg_ln_b), _weight(w_a), _weight(w_b), _weight(w_out), _rows(norm_mix_post)]
    ffn = [_rows(norm_ffn_pre), _weight(w_gate), _weight(w_up), _weight(w_down), _rows(norm_ffn_post),
           _weight(w_ple), _weight(w_ple_gate)]
    p = p.reshape(depth, -1, p.shape[-1])
    for l in range(depth):
        x = _mixer_layer(x, l, *mixer)
        x = _ffn_layer(x, p, l, *ffn)
    return x
```

```python
import functools

import jax
import jax.numpy as jnp
from jax import lax
from jax.experimental import pallas as pl
from jax.experimental.pallas import tpu as pltpu

F32 = jnp.float32
BF16 = jnp.bfloat16

D_MODEL = 1024
HG_HEADS = 8
HG_KEY_DIM = 128
HG_VAL_DIM = 128
HG_KEY = HG_HEADS * HG_KEY_DIM
HG_VAL = HG_HEADS * HG_VAL_DIM
SG_GROUPS = 8
SG_GROUP_DIM = 64
SG_WIDTH = SG_GROUPS * SG_GROUP_DIM
SG_CHUNK = 128
EPS = 1e-6
LOG2E = 1.4426950408889634

MXU_WIDTH = 256
LANES = 128
SUBLANES = 8
BF16_ROWS = 16
PITCH_ALIGN_TILES = 8
CHUNK = 128
N_LEVELS = CHUNK.bit_length() - 1
SMALL_LEVELS = tuple(lg for lg in range(1, N_LEVELS) if (1 << lg) < SUBLANES)
HEADS_PER_PAIR = MXU_WIDTH // HG_KEY_DIM
N_PAIRS = HG_HEADS // HEADS_PER_PAIR

IN_NAMES = ("q", "f_fwd", "f_bwd", "i", "g", "u", "v", "gate_a", "gate_b")
IN_WIDTHS = (HG_KEY, HG_KEY, HG_KEY, HG_VAL, HG_VAL, SG_WIDTH, SG_WIDTH, D_MODEL, D_MODEL)
IN_COLS = {n: slice(sum(IN_WIDTHS[:i]), sum(IN_WIDTHS[:i + 1])) for i, n in enumerate(IN_NAMES)}
IN_SPLIT = IN_COLS["v"].start

MIX_BLOCK = 512
PROJ_ROWS = 256
FFN_BLOCK = 512
VMEM_LIMIT = 52 * 1024 * 1024


def _dot(a, b):
    return jnp.dot(a, b, preferred_element_type=F32)


def _dot_tn(a, b):
    return lax.dot_general(a, b, (((0,), (0,)), ((), ())), preferred_element_type=F32)


def _in_proj(h, w_refs, name):
    c = IN_COLS[name]
    if c.start < IN_SPLIT:
        return _dot(h, w_refs[0][:, c])
    return _dot(h, w_refs[1][:, c.start - IN_SPLIT:c.stop - IN_SPLIT])


def _rms_norm(x, g):
    return x * lax.rsqrt(jnp.mean(x * x, axis=-1, keepdims=True) + EPS) * g


def _sigmoid(x):
    return 1.0 / (1.0 + jnp.exp2(x * -LOG2E))


def _silu(x):
    return x * _sigmoid(x)


def _gelu(x):
    return 0.5 * x * (1.0 + lax.erf(x * (2.0 ** -0.5)))


def _lower_bound(gamma, layer):
    e = jnp.exp(gamma - jnp.max(gamma, axis=0, keepdims=True))
    sm = e / jnp.sum(e, axis=0, keepdims=True)
    lb = jnp.zeros((1, gamma.shape[1]), F32)
    for j in range(1, layer + 1):
        lb = lb + sm[j:j + 1, :]
    return lb


def _forget_gates(zf, lb):
    f_raw = lb + (1.0 - lb) * _sigmoid(zf)
    f = jnp.maximum(f_raw, jnp.finfo(F32).tiny)
    return 1.0 - f_raw, f, jnp.log2(f)


def _split2(x):
    hi = x.astype(BF16)
    return hi, (x - hi.astype(F32)).astype(BF16)


def _block_diag(x):
    w = x.shape[1] // 2
    z = jnp.zeros((x.shape[0], w), x.dtype)
    return jnp.concatenate([jnp.concatenate([x[:, :w], z], axis=1),
                            jnp.concatenate([z, x[:, w:]], axis=1)], axis=0)


def _block_diag_t(x):
    w = x.shape[1] // 2
    xt = x.T
    z = jnp.zeros((w, x.shape[0]), x.dtype)
    return jnp.concatenate([jnp.concatenate([xt[:w], z], axis=1),
                            jnp.concatenate([z, xt[w:]], axis=1)], axis=0)


def _level_weight(b2, lg, reverse):
    m = 1 << lg
    n, w = b2.shape
    b4 = b2.reshape(n // (2 * m), 2, m, w)
    if reverse:
        mid = b4[:, 1:2, 0:1, :]
        e = jnp.concatenate([b4[:, 0:1] - mid, mid - b4[:, 1:2]], axis=1)
    else:
        mid = b4[:, 0:1, m - 1:m, :]
        e = jnp.concatenate([mid - b4[:, 0:1], b4[:, 1:2] - mid], axis=1)
    return jnp.exp2(e.reshape(n, w))


def _query_starts(lg, reverse):
    m = 1 << lg
    return [base + (0 if reverse else m) for base in range(0, CHUNK, 2 * m)]


def _chunk_masks(reverse):
    t = lax.broadcasted_iota(jnp.int32, (CHUNK, MXU_WIDTH), 0)
    s = lax.broadcasted_iota(jnp.int32, (CHUNK, MXU_WIDTH), 1) & (CHUNK - 1)
    x = t ^ s
    order = (t < s) if reverse else (t > s)
    entry = [jnp.logical_and((x >> lg) == 1, order) for lg in range(N_LEVELS) if (1 << lg) < SUBLANES]
    s_tile = lax.broadcasted_iota(jnp.int32, (SUBLANES, MXU_WIDTH), 1) & (CHUNK - 1)
    key_half = {lg: s_tile >> lg for lg in range(N_LEVELS) if (1 << lg) >= SUBLANES}
    return entry, t == s, key_half, _prefix_matrices(reverse)


def _prefix_matrices(reverse):
    t = lax.broadcasted_iota(jnp.int32, (CHUNK, CHUNK), 0)
    u = lax.broadcasted_iota(jnp.int32, (CHUNK, CHUNK), 1)
    mats = [(t <= u) if reverse else (t >= u)]
    for lg in SMALL_LEVELS:
        first_half = ((t >> lg) & 1) == 0
        after_t = (u >= t) if reverse else (u > t)
        mats.append(jnp.logical_and((t >> lg) == (u >> lg), first_half == after_t))
    return jnp.concatenate([jnp.concatenate([m, m], axis=1) for m in mats], axis=0).astype(BF16)


def _level_operands(qp, kp, w, lg, reverse):
    m = 1 << lg
    if m < BF16_ROWS:
        return qp * w, kp * w
    pieces = []
    for s in _query_starts(lg, reverse):
        other = slice(s + m, s + 2 * m) if reverse else slice(s - m, s)
        pieces += [qp[s:s + m], kp[other]] if reverse else [kp[other], qp[s:s + m]]
    xw = jnp.concatenate(pieces, axis=0) * w
    return xw, xw


def _merge_level(acc, xq, xk, lg, reverse, entry, key_half):
    m = 1 << lg
    if m < SUBLANES:
        return jnp.where(entry[lg], _dot(xq, _block_diag_t(xk)), acc)
    starts = _query_starts(lg, reverse)
    if m >= BF16_ROWS:
        s_q = _dot(jnp.concatenate([xq[s:s + m] for s in starts], axis=0), _block_diag_t(xk))
        blocks = [s_q[i * m:(i + 1) * m] for i in range(len(starts))]
    else:
        s_all = _dot(xq, _block_diag_t(xk))
        blocks = [s_all[s:s + m] for s in starts]
    pieces = []
    for s, blk in zip(starts, blocks):
        keys = (s >> lg) + (1 if reverse else -1)
        tiles = (m // SUBLANES, SUBLANES, MXU_WIDTH)
        merged = jnp.where((key_half[lg] == keys)[None], blk.reshape(tiles),
                           acc[s:s + m].reshape(tiles)).reshape(m, MXU_WIDTH)
        other = acc[s + m:s + 2 * m] if reverse else acc[s - m:s]
        pieces += [merged, other] if reverse else [other, merged]
    return jnp.concatenate(pieces, axis=0)


def _hgrn2_chunk(q, k, f, log2f, v, state_ref, reverse, masks):
    entry, diag_mask, key_half, prefix = masks
    sums = _dot(prefix, jnp.concatenate(_split2(log2f), axis=0))
    b2_all = sums[:CHUNK]
    small_exponents = {lg: sums[(i + 1) * CHUNK:(i + 2) * CHUNK] for i, lg in enumerate(SMALL_LEVELS)}
    qf = q * f.astype(BF16)
    edge = 0 if reverse else CHUNK - 1

    outs = []
    for p in range(N_PAIRS):
        sl = slice(p * MXU_WIDTH, (p + 1) * MXU_WIDTH)
        qp, kp, b2, vp = q[:, sl], k[:, sl], b2_all[:, sl], v[:, sl]

        qk = (qp * kp).astype(F32)
        d_pair = [jnp.broadcast_to(jnp.sum(qk[:, j * HG_KEY_DIM:(j + 1) * HG_KEY_DIM], axis=1, keepdims=True),
                                   (CHUNK, CHUNK)) for j in range(HEADS_PER_PAIR)]
        acc = jnp.where(diag_mask, jnp.concatenate(d_pair, axis=1), 0.0)
        for lg in range(N_LEVELS):
            if lg == 0:
                xq, xk = qf[:, sl], kp
            else:
                w = jnp.exp2(small_exponents[lg][:, sl]) if lg in SMALL_LEVELS else _level_weight(b2, lg, reverse)
                xq, xk = _level_operands(qp, kp, w.astype(BF16), lg, reverse)
            acc = _merge_level(acc, xq, xk, lg, reverse, entry, key_half)
        o_intra = _dot(acc.astype(BF16), _block_diag(vp))

        h0 = p * HEADS_PER_PAIR
        st = [state_ref[h0 + j] for j in range(HEADS_PER_PAIR)]
        z = jnp.zeros((HG_VAL_DIM, HG_KEY_DIM), BF16)
        st_bd = jnp.concatenate([jnp.concatenate([st[0].astype(BF16).T, z], axis=1),
                                 jnp.concatenate([z, st[1].astype(BF16).T], axis=1)], axis=0)
        o_inter = _dot(qp * jnp.exp2(b2).astype(BF16), st_bd)
        outs.append(o_intra + o_inter)

        b2_edge = b2[edge:edge + 1, :]
        k_to_edge = kp * jnp.exp2(b2_edge - b2).astype(BF16)
        decay = jnp.exp2(b2_edge)
        for j in range(HEADS_PER_PAIR):
            hs = slice(j * HG_KEY_DIM, (j + 1) * HG_KEY_DIM)
            state_ref[h0 + j] = st[j] * decay[:, hs] + _dot_tn(vp[:, hs], k_to_edge[:, hs])
    return jnp.concatenate(outs, axis=1)


def _bwd_pass_kernel(x_ref, g_pre_ref, w_ref, gamma_ref, q_ref, v_ref, o_ref, state_ref, *, layer):
    @pl.when(pl.program_id(1) == 0)
    def _():
        state_ref[...] = jnp.zeros_like(state_ref)

    masks = _chunk_masks(True)
    lb = _lower_bound(gamma_ref[...], layer)
    for sb in reversed(range(MIX_BLOCK // PROJ_ROWS)):
        base = sb * PROJ_ROWS
        h = _rms_norm(x_ref[0, base:base + PROJ_ROWS, :], g_pre_ref[...]).astype(BF16)
        q = _silu(_in_proj(h, (w_ref,), "q")).astype(BF16)
        q_ref[0, base:base + PROJ_ROWS, :] = q
        v = _in_proj(h, (w_ref,), "i").astype(BF16)
        v_ref[0, base:base + PROJ_ROWS, :] = v
        k, f, log2f = _forget_gates(_in_proj(h, (w_ref,), "f_bwd"), lb)
        k = k.astype(BF16)
        for c in reversed(range(PROJ_ROWS // CHUNK)):
            rows = slice(c * CHUNK, (c + 1) * CHUNK)
            o_ref[0, base + c * CHUNK:base + (c + 1) * CHUNK, :] = _hgrn2_chunk(
                q[rows], k[rows], f[rows], log2f[rows], v[rows], state_ref, True, masks).astype(BF16)


def _fwd_pass_kernel(x_ref, q_ref, v_ref, ob_ref, g_pre_ref, w_ref, w_hi_ref, gamma_ref, hg_norm_ref, sg_w_ref,
                     sg_bias_ref, ln_g_ref, ln_b_ref, w_a_ref, w_b_ref, w_out_ref, g_post_ref,
                     out_ref, state_ref, *, layer):
    @pl.when(pl.program_id(1) == 0)
    def _():
        state_ref[...] = jnp.zeros_like(state_ref)

    masks = _chunk_masks(False)
    lb = _lower_bound(gamma_ref[...], layer)

    for sb in range(MIX_BLOCK // PROJ_ROWS):
        blk = slice(sb * PROJ_ROWS, (sb + 1) * PROJ_ROWS)
        x = x_ref[0, blk, :]
        h = _rms_norm(x, g_pre_ref[...]).astype(BF16)

        k, f, log2f = _forget_gates(_in_proj(h, (w_ref, w_hi_ref), "f_fwd"), lb)
        k = k.astype(BF16)
        q = q_ref[0, blk, :]
        v = v_ref[0, blk, :]
        o_chunks = []
        for c in range(PROJ_ROWS // CHUNK):
            rows = slice(c * CHUNK, (c + 1) * CHUNK)
            o_chunks.append(_hgrn2_chunk(q[rows], k[rows], f[rows], log2f[rows], v[rows], state_ref, False, masks))
        o = jnp.concatenate(o_chunks, axis=0) + ob_ref[0, blk, :].astype(F32)
        heads = []
        for hd in range(HG_HEADS):
            oh = o[:, hd * HG_VAL_DIM:(hd + 1) * HG_VAL_DIM]
            heads.append(oh * lax.rsqrt(jnp.mean(oh * oh, axis=-1, keepdims=True) + EPS))
        a_out = jnp.concatenate(heads, axis=1) * hg_norm_ref[...] * _silu(_in_proj(h, (w_ref, w_hi_ref), "g"))
        a_proj = _dot(a_out.astype(BF16), w_a_ref[:, :D_MODEL])

        u = _gelu(_in_proj(h, (w_ref, w_hi_ref), "u"))
        gv = _gelu(_in_proj(h, (w_ref, w_hi_ref), "v"))
        mu = jnp.mean(gv, axis=-1, keepdims=True)
        gc = gv - mu
        vn = (gc * lax.rsqrt(jnp.mean(gc * gc, axis=-1, keepdims=True) + EPS) * ln_g_ref[...]
              + ln_b_ref[...]).astype(BF16)
        sg_rows = []
        for c in range(PROJ_ROWS // SG_CHUNK):
            rows = slice(c * SG_CHUNK, (c + 1) * SG_CHUNK)
            cols = [_dot(sg_w_ref[gp], _block_diag(vn[rows, gp * LANES:(gp + 1) * LANES]))
                    for gp in range(SG_WIDTH // LANES)]
            sg_rows.append(jnp.concatenate(cols, axis=1) + sg_bias_ref[...])
        b_out = u * jnp.concatenate(sg_rows, axis=0)
        b_proj = _dot(b_out.astype(BF16), w_b_ref[:, :D_MODEL])

        merged = (_sigmoid(_in_proj(h, (w_ref, w_hi_ref), "gate_a")) * a_proj
                  + _sigmoid(_in_proj(h, (w_ref, w_hi_ref), "gate_b")) * b_proj)
        mix = _dot(merged.astype(BF16), w_out_ref[:, :D_MODEL])
        out_ref[0, blk, :] = x + _rms_norm(mix, g_post_ref[...])


def _ffn_pass_kernel(x_ref, p_ref, g_pre_ref, w_gate_ref, w_up_ref, w_down_ref, g_post_ref,
                     w_ple_ref, w_ple_gate_ref, out_ref):
    x = x_ref[...]
    h = _rms_norm(x, g_pre_ref[...]).astype(BF16)
    act = _silu(_dot(h, w_gate_ref[...])) * _dot(h, w_up_ref[...])
    ff = _dot(act.astype(BF16), w_down_ref[:, :D_MODEL])
    x = x + _rms_norm(ff, g_post_ref[...])
    emb = _dot(p_ref[...].astype(BF16), w_ple_ref[:, :D_MODEL])
    out_ref[...] = x + emb * _sigmoid(_dot(x.astype(BF16), w_ple_gate_ref[:, :D_MODEL]))


def _resident(shape):
    return pl.BlockSpec(shape, lambda *_: (0,) * len(shape), pipeline_mode=pl.Buffered(1))


def _layer_of(a, layer):
    rest = a.shape[1:]
    return pl.BlockSpec((None,) + rest, lambda *_: (layer,) + (0,) * len(rest), pipeline_mode=pl.Buffered(1))


def _rows(a):
    return a.reshape(a.shape[0], 1, -1).astype(F32)


def _weight(w):
    w = w.astype(BF16)
    if (w.shape[-1] // LANES) % PITCH_ALIGN_TILES == 0:
        w = jnp.pad(w, ((0, 0),) * (w.ndim - 1) + ((0, LANES),))
    return w


def _mixer_layer(x, layer, norm_pre, w_in, w_in_hi, gamma_f, gamma_b, hg_norm, sg_w_pairs, sg_bias, ln_g, ln_b,
                 w_a, w_b, w_out, norm_post):
    bsz, seq, d = x.shape
    nblk = seq // MIX_BLOCK
    scratch = [pltpu.VMEM((HG_HEADS, HG_VAL_DIM, HG_KEY_DIM), F32)]
    params = pltpu.CompilerParams(dimension_semantics=("arbitrary", "arbitrary"), vmem_limit_bytes=VMEM_LIMIT)
    tok = lambda width, rev: pl.BlockSpec(
        (1, MIX_BLOCK, width), (lambda b, i: (b, nblk - 1 - i, 0)) if rev else (lambda b, i: (b, i, 0)))

    q, v, o_bwd = pl.pallas_call(
        functools.partial(_bwd_pass_kernel, layer=layer),
        grid=(bsz, nblk),
        in_specs=[tok(d, True), _layer_of(norm_pre, layer), _layer_of(w_in, layer), _resident(gamma_b.shape)],
        out_specs=[tok(HG_KEY, True), tok(HG_VAL, True), tok(HG_VAL, True)],
        out_shape=[jax.ShapeDtypeStruct((bsz, seq, HG_KEY), BF16),
                   jax.ShapeDtypeStruct((bsz, seq, HG_VAL), BF16),
                   jax.ShapeDtypeStruct((bsz, seq, HG_VAL), BF16)],
        scratch_shapes=scratch,
        compiler_params=params,
        name=f"hgrn2_bwd_pass_l{layer}",
    )(x, norm_pre, w_in, gamma_b)

    stacked = [norm_pre, w_in, w_in_hi, None, hg_norm, sg_w_pairs, sg_bias, ln_g, ln_b, w_a, w_b, w_out, norm_post]
    specs = [_resident(gamma_f.shape) if a is None else _layer_of(a, layer) for a in stacked]
    stacked[3] = gamma_f
    return pl.pallas_call(
        functools.partial(_fwd_pass_kernel, layer=layer),
        grid=(bsz, nblk),
        in_specs=[tok(d, False), tok(HG_KEY, False), tok(HG_VAL, False), tok(HG_VAL, False)] + specs,
        out_specs=tok(d, False),
        out_shape=jax.ShapeDtypeStruct((bsz, seq, d), F32),
        scratch_shapes=scratch,
        compiler_params=params,
        name=f"hgrn2_fwd_mix_pass_l{layer}",
    )(x, q, v, o_bwd, *stacked)


def _ffn_layer(x, p, layer, norm_pre, w_gate, w_up, w_down, norm_post, w_ple, w_ple_gate):
    bsz, seq, d = x.shape
    tokens = bsz * seq
    stacked = [norm_pre, w_gate, w_up, w_down, norm_post, w_ple, w_ple_gate]
    tok = lambda width: pl.BlockSpec((FFN_BLOCK, width), lambda i: (i, 0))
    out = pl.pallas_call(
        _ffn_pass_kernel,
        grid=(tokens // FFN_BLOCK,),
        in_specs=[tok(d), pl.BlockSpec((None, FFN_BLOCK, p.shape[-1]), lambda i: (layer, i, 0))]
        + [_layer_of(a, layer) for a in stacked],
        out_specs=tok(d),
        out_shape=jax.ShapeDtypeStruct((tokens, d), F32),
        compiler_params=pltpu.CompilerParams(dimension_semantics=("arbitrary",), vmem_limit_bytes=VMEM_LIMIT),
        name=f"swiglu_ple_pass_l{layer}",
    )(x.reshape(tokens, d), p, *stacked)
    return out.reshape(bsz, seq, d)


def kernel(x, p, norm_mix_pre, w_in, lb_gamma_fwd, lb_gamma_bwd, hg_norm, sg_w, sg_b, sg_ln_g, sg_ln_b, w_a, w_b, w_out, norm_mix_post, norm_ffn_pre, w_gate, w_up, w_down, norm_ffn_post, w_ple, w_ple_gate):
    assert x.shape[1] % MIX_BLOCK == 0 and (x.shape[0] * x.shape[1]) % FFN_BLOCK == 0
    depth = w_in.shape[0]
    sg_w_pairs = sg_w.reshape(depth, SG_GROUPS // 2, 2, SG_CHUNK, SG_CHUNK).transpose(0, 1, 3, 2, 4).reshape(
        depth, SG_GROUPS // 2, SG_CHUNK, 2 * SG_CHUNK).astype(BF16)
    sg_bias = jnp.repeat(sg_b.transpose(0, 2, 1), SG_GROUP_DIM, axis=2).astype(F32)
    mixer = [_rows(norm_mix_pre), _weight(w_in[..., :IN_SPLIT]), _weight(w_in[..., IN_SPLIT:]), lb_gamma_fwd, lb_gamma_bwd, _rows(hg_norm), sg_w_pairs, sg_bias,
             _rows(sg_ln_g), _rows(sg_ln_b), _weight(w_a), _weight(w_b), _weight(w_out), _rows(norm_mix_post)]
    ffn = [_rows(norm_ffn_pre), _weight(w_gate), _weight(w_up), _weight(w_down), _rows(norm_ffn_post),
           _weight(w_ple), _weight(w_ple_gate)]
    p = p.reshape(depth, -1, p.shape[-1])
    for l in range(depth):
        x = _mixer_layer(x, l, *mixer)
        x = _ffn_layer(x, p, l, *ffn)
    return x
```

```python
import functools

import jax
import jax.numpy as jnp
from jax import lax
from jax.experimental import pallas as pl
from jax.experimental.pallas import tpu as pltpu

F32 = jnp.float32
BF16 = jnp.bfloat16

D_MODEL = 1024
HG_HEADS = 8
HG_KEY_DIM = 128
HG_VAL_DIM = 128
HG_KEY = HG_HEADS * HG_KEY_DIM
HG_VAL = HG_HEADS * HG_VAL_DIM
SG_GROUPS = 8
SG_GROUP_DIM = 64
SG_WIDTH = SG_GROUPS * SG_GROUP_DIM
SG_CHUNK = 128
EPS = 1e-6
LOG2E = 1.4426950408889634

MXU_WIDTH = 256
LANES = 128
SUBLANES = 8
BF16_ROWS = 16
PITCH_ALIGN_TILES = 8
CHUNK = 128
N_LEVELS = CHUNK.bit_length() - 1
SMALL_LEVELS = tuple(lg for lg in range(1, N_LEVELS) if (1 << lg) < SUBLANES)
HEADS_PER_PAIR = MXU_WIDTH // HG_KEY_DIM
N_PAIRS = HG_HEADS // HEADS_PER_PAIR

IN_NAMES = ("q", "f_fwd", "f_bwd", "i", "g", "u", "v", "gate_a", "gate_b")
IN_WIDTHS = (HG_KEY, HG_KEY, HG_KEY, HG_VAL, HG_VAL, SG_WIDTH, SG_WIDTH, D_MODEL, D_MODEL)
IN_COLS = {n: slice(sum(IN_WIDTHS[:i]), sum(IN_WIDTHS[:i + 1])) for i, n in enumerate(IN_NAMES)}

MIX_BLOCK = 512
PROJ_ROWS = 256
FFN_BLOCK = 512
VMEM_LIMIT = 52 * 1024 * 1024


def _dot(a, b):
    return jnp.dot(a, b, preferred_element_type=F32)


def _dot_tn(a, b):
    return lax.dot_general(a, b, (((0,), (0,)), ((), ())), preferred_element_type=F32)


def _rms_norm(x, g):
    return x * lax.rsqrt(jnp.mean(x * x, axis=-1, keepdims=True) + EPS) * g


def _sigmoid(x):
    return 1.0 / (1.0 + jnp.exp2(x * -LOG2E))


def _silu(x):
    return x * _sigmoid(x)


def _gelu(x):
    return 0.5 * x * (1.0 + lax.erf(x * (2.0 ** -0.5)))


def _lower_bound(gamma, layer):
    e = jnp.exp(gamma - jnp.max(gamma, axis=0, keepdims=True))
    sm = e / jnp.sum(e, axis=0, keepdims=True)
    lb = jnp.zeros((1, gamma.shape[1]), F32)
    for j in range(1, layer + 1):
        lb = lb + sm[j:j + 1, :]
    return lb


def _forget_gates(zf, lb):
    f_raw = lb + (1.0 - lb) * _sigmoid(zf)
    f = jnp.maximum(f_raw, jnp.finfo(F32).tiny)
    return 1.0 - f_raw, f, jnp.log2(f)


def _split2(x):
    hi = x.astype(BF16)
    return hi, (x - hi.astype(F32)).astype(BF16)


def _block_diag(x):
    w = x.shape[1] // 2
    z = jnp.zeros((x.shape[0], w), x.dtype)
    return jnp.concatenate([jnp.concatenate([x[:, :w], z], axis=1),
                            jnp.concatenate([z, x[:, w:]], axis=1)], axis=0)


def _block_diag_t(x):
    w = x.shape[1] // 2
    xt = x.T
    z = jnp.zeros((w, x.shape[0]), x.dtype)
    return jnp.concatenate([jnp.concatenate([xt[:w], z], axis=1),
                            jnp.concatenate([z, xt[w:]], axis=1)], axis=0)


def _level_weight(b2, lg, reverse):
    m = 1 << lg
    n, w = b2.shape
    b4 = b2.reshape(n // (2 * m), 2, m, w)
    if reverse:
        mid = b4[:, 1:2, 0:1, :]
        e = jnp.concatenate([b4[:, 0:1] - mid, mid - b4[:, 1:2]], axis=1)
    else:
        mid = b4[:, 0:1, m - 1:m, :]
        e = jnp.concatenate([mid - b4[:, 0:1], b4[:, 1:2] - mid], axis=1)
    return jnp.exp2(e.reshape(n, w))


def _query_starts(lg, reverse):
    m = 1 << lg
    return [base + (0 if reverse else m) for base in range(0, CHUNK, 2 * m)]


def _chunk_masks(reverse):
    t = lax.broadcasted_iota(jnp.int32, (CHUNK, MXU_WIDTH), 0)
    s = lax.broadcasted_iota(jnp.int32, (CHUNK, MXU_WIDTH), 1) & (CHUNK - 1)
    x = t ^ s
    order = (t < s) if reverse else (t > s)
    entry = [jnp.logical_and((x >> lg) == 1, order) for lg in range(N_LEVELS) if (1 << lg) < SUBLANES]
    s_tile = lax.broadcasted_iota(jnp.int32, (SUBLANES, MXU_WIDTH), 1) & (CHUNK - 1)
    key_half = {lg: s_tile >> lg for lg in range(N_LEVELS) if (1 << lg) >= SUBLANES}
    return entry, t == s, key_half, _prefix_matrices(reverse)


def _prefix_matrices(reverse):
    t = lax.broadcasted_iota(jnp.int32, (CHUNK, CHUNK), 0)
    u = lax.broadcasted_iota(jnp.int32, (CHUNK, CHUNK), 1)
    mats = [(t <= u) if reverse else (t >= u)]
    for lg in SMALL_LEVELS:
        first_half = ((t >> lg) & 1) == 0
        after_t = (u >= t) if reverse else (u > t)
        mats.append(jnp.logical_and((t >> lg) == (u >> lg), first_half == after_t))
    return jnp.concatenate([jnp.concatenate([m, m], axis=1) for m in mats], axis=0).astype(BF16)


def _level_operands(qp, kp, w, lg, reverse):
    m = 1 << lg
    if m < BF16_ROWS:
        return qp * w, kp * w
    pieces = []
    for s in _query_starts(lg, reverse):
        other = slice(s + m, s + 2 * m) if reverse else slice(s - m, s)
        pieces += [qp[s:s + m], kp[other]] if reverse else [kp[other], qp[s:s + m]]
    xw = jnp.concatenate(pieces, axis=0) * w
    return xw, xw


def _merge_level(acc, xq, xk, lg, reverse, entry, key_half):
    m = 1 << lg
    if m < SUBLANES:
        return jnp.where(entry[lg], _dot(xq, _block_diag_t(xk)), acc)
    starts = _query_starts(lg, reverse)
    if m >= BF16_ROWS:
        s_q = _dot(jnp.concatenate([xq[s:s + m] for s in starts], axis=0), _block_diag_t(xk))
        blocks = [s_q[i * m:(i + 1) * m] for i in range(len(starts))]
    else:
        s_all = _dot(xq, _block_diag_t(xk))
        blocks = [s_all[s:s + m] for s in starts]
    pieces = []
    for s, blk in zip(starts, blocks):
        keys = (s >> lg) + (1 if reverse else -1)
        tiles = (m // SUBLANES, SUBLANES, MXU_WIDTH)
        merged = jnp.where((key_half[lg] == keys)[None], blk.reshape(tiles),
                           acc[s:s + m].reshape(tiles)).reshape(m, MXU_WIDTH)
        other = acc[s + m:s + 2 * m] if reverse else acc[s - m:s]
        pieces += [merged, other] if reverse else [other, merged]
    return jnp.concatenate(pieces, axis=0)


def _hgrn2_chunk(q, k, f, log2f, v, state_ref, reverse, masks):
    entry, diag_mask, key_half, prefix = masks
    sums = _dot(prefix, jnp.concatenate(_split2(log2f), axis=0))
    b2_all = sums[:CHUNK]
    small_exponents = {lg: sums[(i + 1) * CHUNK:(i + 2) * CHUNK] for i, lg in enumerate(SMALL_LEVELS)}
    qf = q * f.astype(BF16)
    edge = 0 if reverse else CHUNK - 1

    outs = []
    for p in range(N_PAIRS):
        sl = slice(p * MXU_WIDTH, (p + 1) * MXU_WIDTH)
        qp, kp, b2, vp = q[:, sl], k[:, sl], b2_all[:, sl], v[:, sl]

        qk = (qp * kp).astype(F32)
        d_pair = [jnp.broadcast_to(jnp.sum(qk[:, j * HG_KEY_DIM:(j + 1) * HG_KEY_DIM], axis=1, keepdims=True),
                                   (CHUNK, CHUNK)) for j in range(HEADS_PER_PAIR)]
        acc = jnp.where(diag_mask, jnp.concatenate(d_pair, axis=1), 0.0)
        for lg in range(N_LEVELS):
            if lg == 0:
                xq, xk = qf[:, sl], kp
            else:
                w = jnp.exp2(small_exponents[lg][:, sl]) if lg in SMALL_LEVELS else _level_weight(b2, lg, reverse)
                xq, xk = _level_operands(qp, kp, w.astype(BF16), lg, reverse)
            acc = _merge_level(acc, xq, xk, lg, reverse, entry, key_half)
        o_intra = _dot(acc.astype(BF16), _block_diag(vp))

        h0 = p * HEADS_PER_PAIR
        st = [state_ref[h0 + j] for j in range(HEADS_PER_PAIR)]
        z = jnp.zeros((HG_VAL_DIM, HG_KEY_DIM), BF16)
        st_bd = jnp.concatenate([jnp.concatenate([st[0].astype(BF16).T, z], axis=1),
                                 jnp.concatenate([z, st[1].astype(BF16).T], axis=1)], axis=0)
        o_inter = _dot(qp * jnp.exp2(b2).astype(BF16), st_bd)
        outs.append(o_intra + o_inter)

        b2_edge = b2[edge:edge + 1, :]
        k_to_edge = kp * jnp.exp2(b2_edge - b2).astype(BF16)
        decay = jnp.exp2(b2_edge)
        for j in range(HEADS_PER_PAIR):
            hs = slice(j * HG_KEY_DIM, (j + 1) * HG_KEY_DIM)
            state_ref[h0 + j] = st[j] * decay[:, hs] + _dot_tn(vp[:, hs], k_to_edge[:, hs])
    return jnp.concatenate(outs, axis=1)


def _bwd_pass_kernel(x_ref, g_pre_ref, w_ref, gamma_ref, q_ref, v_ref, o_ref, state_ref, *, layer):
    @pl.when(pl.program_id(1) == 0)
    def _():
        state_ref[...] = jnp.zeros_like(state_ref)

    masks = _chunk_masks(True)
    lb = _lower_bound(gamma_ref[...], layer)
    for sb in reversed(range(MIX_BLOCK // PROJ_ROWS)):
        base = sb * PROJ_ROWS
        h = _rms_norm(x_ref[0, base:base + PROJ_ROWS, :], g_pre_ref[...]).astype(BF16)
        q = _silu(_dot(h, w_ref[:, IN_COLS["q"]])).astype(BF16)
        q_ref[0, base:base + PROJ_ROWS, :] = q
        v = _dot(h, w_ref[:, IN_COLS["i"]]).astype(BF16)
        v_ref[0, base:base + PROJ_ROWS, :] = v
        k, f, log2f = _forget_gates(_dot(h, w_ref[:, IN_COLS["f_bwd"]]), lb)
        k = k.astype(BF16)
        for c in reversed(range(PROJ_ROWS // CHUNK)):
            rows = slice(c * CHUNK, (c + 1) * CHUNK)
            o_ref[0, base + c * CHUNK:base + (c + 1) * CHUNK, :] = _hgrn2_chunk(
                q[rows], k[rows], f[rows], log2f[rows], v[rows], state_ref, True, masks)


def _fwd_pass_kernel(x_ref, q_ref, v_ref, ob_ref, g_pre_ref, w_ref, gamma_ref, hg_norm_ref, a_ref, state_ref, *,
                     layer):
    @pl.when(pl.program_id(1) == 0)
    def _():
        state_ref[...] = jnp.zeros_like(state_ref)

    masks = _chunk_masks(False)
    lb = _lower_bound(gamma_ref[...], layer)
    for sb in range(MIX_BLOCK // PROJ_ROWS):
        blk = slice(sb * PROJ_ROWS, (sb + 1) * PROJ_ROWS)
        h = _rms_norm(x_ref[0, blk, :], g_pre_ref[...]).astype(BF16)
        k, f, log2f = _forget_gates(_dot(h, w_ref[:, IN_COLS["f_fwd"]]), lb)
        k = k.astype(BF16)
        q = q_ref[0, blk, :]
        v = v_ref[0, blk, :]
        o_chunks = []
        for c in range(PROJ_ROWS // CHUNK):
            rows = slice(c * CHUNK, (c + 1) * CHUNK)
            o_chunks.append(_hgrn2_chunk(q[rows], k[rows], f[rows], log2f[rows], v[rows], state_ref, False, masks))
        o = jnp.concatenate(o_chunks, axis=0) + ob_ref[0, blk, :]
        heads = []
        for hd in range(HG_HEADS):
            oh = o[:, hd * HG_VAL_DIM:(hd + 1) * HG_VAL_DIM]
            heads.append(oh * lax.rsqrt(jnp.mean(oh * oh, axis=-1, keepdims=True) + EPS))
        a_ref[0, blk, :] = (jnp.concatenate(heads, axis=1) * hg_norm_ref[...]).astype(BF16)


def _mix_pass_kernel(x_ref, a_ref, g_pre_ref, w_ref, sg_w_ref, sg_bias_ref, ln_g_ref, ln_b_ref, w_a_ref, w_b_ref,
                     w_out_ref, g_post_ref, out_ref):
    x = x_ref[...]
    h = _rms_norm(x, g_pre_ref[...]).astype(BF16)
    a_out = a_ref[...].astype(F32) * _silu(_dot(h, w_ref[:, IN_COLS["g"]]))
    a_proj = _dot(a_out.astype(BF16), w_a_ref[:, :D_MODEL])

    u = _gelu(_dot(h, w_ref[:, IN_COLS["u"]]))
    gv = _gelu(_dot(h, w_ref[:, IN_COLS["v"]]))
    mu = jnp.mean(gv, axis=-1, keepdims=True)
    gc = gv - mu
    vn = (gc * lax.rsqrt(jnp.mean(gc * gc, axis=-1, keepdims=True) + EPS) * ln_g_ref[...]
          + ln_b_ref[...]).astype(BF16)
    sg_rows = []
    for c in range(MIX_BLOCK // SG_CHUNK):
        rows = slice(c * SG_CHUNK, (c + 1) * SG_CHUNK)
        cols = [_dot(sg_w_ref[gp], _block_diag(vn[rows, gp * LANES:(gp + 1) * LANES]))
                for gp in range(SG_WIDTH // LANES)]
        sg_rows.append(jnp.concatenate(cols, axis=1) + sg_bias_ref[...])
    b_out = u * jnp.concatenate(sg_rows, axis=0)
    b_proj = _dot(b_out.astype(BF16), w_b_ref[:, :D_MODEL])

    merged = (_sigmoid(_dot(h, w_ref[:, IN_COLS["gate_a"]])) * a_proj
              + _sigmoid(_dot(h, w_ref[:, IN_COLS["gate_b"]])) * b_proj)
    mix = _dot(merged.astype(BF16), w_out_ref[:, :D_MODEL])
    out_ref[...] = x + _rms_norm(mix, g_post_ref[...])


def _ffn_pass_kernel(x_ref, p_ref, g_pre_ref, w_gate_ref, w_up_ref, w_down_ref, g_post_ref,
                     w_ple_ref, w_ple_gate_ref, out_ref):
    x = x_ref[...]
    h = _rms_norm(x, g_pre_ref[...]).astype(BF16)
    act = _silu(_dot(h, w_gate_ref[...])) * _dot(h, w_up_ref[...])
    ff = _dot(act.astype(BF16), w_down_ref[:, :D_MODEL])
    x = x + _rms_norm(ff, g_post_ref[...])
    emb = _dot(p_ref[...].astype(BF16), w_ple_ref[:, :D_MODEL])
    out_ref[...] = x + emb * _sigmoid(_dot(x.astype(BF16), w_ple_gate_ref[:, :D_MODEL]))


def _resident(shape):
    return pl.BlockSpec(shape, lambda *_: (0,) * len(shape), pipeline_mode=pl.Buffered(1))


def _layer_of(a, layer):
    rest = a.shape[1:]
    return pl.BlockSpec((None,) + rest, lambda *_: (layer,) + (0,) * len(rest), pipeline_mode=pl.Buffered(1))


def _rows(a):
    return a.reshape(a.shape[0], 1, -1).astype(F32)


def _weight(w):
    w = w.astype(BF16)
    if (w.shape[-1] // LANES) % PITCH_ALIGN_TILES == 0:
        w = jnp.pad(w, ((0, 0),) * (w.ndim - 1) + ((0, LANES),))
    return w


def _mixer_layer(x, layer, norm_pre, w_in, gamma_f, gamma_b, hg_norm, sg_w_pairs, sg_bias, ln_g, ln_b,
                 w_a, w_b, w_out, norm_post):
    bsz, seq, d = x.shape
    nblk = seq // MIX_BLOCK
    scratch = [pltpu.VMEM((HG_HEADS, HG_VAL_DIM, HG_KEY_DIM), F32)]
    params = pltpu.CompilerParams(dimension_semantics=("arbitrary", "arbitrary"), vmem_limit_bytes=VMEM_LIMIT)
    tok = lambda width, rev: pl.BlockSpec(
        (1, MIX_BLOCK, width), (lambda b, i: (b, nblk - 1 - i, 0)) if rev else (lambda b, i: (b, i, 0)))

    q, v, o_bwd = pl.pallas_call(
        functools.partial(_bwd_pass_kernel, layer=layer),
        grid=(bsz, nblk),
        in_specs=[tok(d, True), _layer_of(norm_pre, layer), _layer_of(w_in, layer), _resident(gamma_b.shape)],
        out_specs=[tok(HG_KEY, True), tok(HG_VAL, True), tok(HG_VAL, True)],
        out_shape=[jax.ShapeDtypeStruct((bsz, seq, HG_KEY), BF16),
                   jax.ShapeDtypeStruct((bsz, seq, HG_VAL), BF16),
                   jax.ShapeDtypeStruct((bsz, seq, HG_VAL), F32)],
        scratch_shapes=scratch,
        compiler_params=params,
        name=f"hgrn2_bwd_pass_l{layer}",
    )(x, norm_pre, w_in, gamma_b)

    a_norm = pl.pallas_call(
        functools.partial(_fwd_pass_kernel, layer=layer),
        grid=(bsz, nblk),
        in_specs=[tok(d, False), tok(HG_KEY, False), tok(HG_VAL, False), tok(HG_VAL, False),
                  _layer_of(norm_pre, layer), _layer_of(w_in, layer), _resident(gamma_f.shape),
                  _layer_of(hg_norm, layer)],
        out_specs=tok(HG_VAL, False),
        out_shape=jax.ShapeDtypeStruct((bsz, seq, HG_VAL), BF16),
        scratch_shapes=scratch,
        compiler_params=params,
        name=f"hgrn2_fwd_pass_l{layer}",
    )(x, q, v, o_bwd, norm_pre, w_in, gamma_f, hg_norm)

    tokens = bsz * seq
    stacked = [norm_pre, w_in, sg_w_pairs, sg_bias, ln_g, ln_b, w_a, w_b, w_out, norm_post]
    flat = lambda width: pl.BlockSpec((MIX_BLOCK, width), lambda i: (i, 0))
    out = pl.pallas_call(
        _mix_pass_kernel,
        grid=(tokens // MIX_BLOCK,),
        in_specs=[flat(d), flat(HG_VAL)] + [_layer_of(a, layer) for a in stacked],
        out_specs=flat(d),
        out_shape=jax.ShapeDtypeStruct((tokens, d), F32),
        compiler_params=pltpu.CompilerParams(dimension_semantics=("arbitrary",), vmem_limit_bytes=VMEM_LIMIT),
        name=f"mixer_dense_pass_l{layer}",
    )(x.reshape(tokens, d), a_norm.reshape(tokens, HG_VAL), *stacked)
    return out.reshape(bsz, seq, d)


def _ffn_layer(x, p, layer, norm_pre, w_gate, w_up, w_down, norm_post, w_ple, w_ple_gate):
    bsz, seq, d = x.shape
    tokens = bsz * seq
    stacked = [norm_pre, w_gate, w_up, w_down, norm_post, w_ple, w_ple_gate]
    tok = lambda width: pl.BlockSpec((FFN_BLOCK, width), lambda i: (i, 0))
    out = pl.pallas_call(
        _ffn_pass_kernel,
        grid=(tokens // FFN_BLOCK,),
        in_specs=[tok(d), pl.BlockSpec((None, FFN_BLOCK, p.shape[-1]), lambda i: (layer, i, 0))]
        + [_layer_of(a, layer) for a in stacked],
        out_specs=tok(d),
        out_shape=jax.ShapeDtypeStruct((tokens, d), F32),
        compiler_params=pltpu.CompilerParams(dimension_semantics=("arbitrary",), vmem_limit_bytes=VMEM_LIMIT),
        name=f"swiglu_ple_pass_l{layer}",
    )(x.reshape(tokens, d), p, *stacked)
    return out.reshape(bsz, seq, d)


def kernel(x, p, norm_mix_pre, w_in, lb_gamma_fwd, lb_gamma_bwd, hg_norm, sg_w, sg_b, sg_ln_g, sg_ln_b, w_a, w_b, w_out, norm_mix_post, norm_ffn_pre, w_gate, w_up, w_down, norm_ffn_post, w_ple, w_ple_gate):
    assert x.shape[1] % MIX_BLOCK == 0 and (x.shape[0] * x.shape[1]) % FFN_BLOCK == 0
    depth = w_in.shape[0]
    sg_w_pairs = sg_w.reshape(depth, SG_GROUPS // 2, 2, SG_CHUNK, SG_CHUNK).transpose(0, 1, 3, 2, 4).reshape(
        depth, SG_GROUPS // 2, SG_CHUNK, 2 * SG_CHUNK).astype(BF16)
    sg_bias = jnp.repeat(sg_b.transpose(0, 2, 1), SG_GROUP_DIM, axis=2).astype(F32)
    mixer = [_rows(norm_mix_pre), _weight(w_in), lb_gamma_fwd, lb_gamma_bwd, _rows(hg_norm), sg_w_pairs, sg_bias,
             _rows(sg_ln_g), _rows(sg_ln_b), _weight(w_a), _weight(w_b), _weight(w_out), _rows(norm_mix_post)]
    ffn = [_rows(norm_ffn_pre), _weight(w_gate), _weight(w_up), _weight(w_down), _rows(norm_ffn_post),
           _weight(w_ple), _weight(w_ple_gate)]
    p = p.reshape(depth, -1, p.shape[-1])
    for l in range(depth):
        x = _mixer_layer(x, l, *mixer)
        x = _ffn_layer(x, p, l, *ffn)
    return x
```
